```python
import jax, jax.numpy as jnp
from jax import lax
import numpy as np

D_MODEL = 1024
BATCH = 16
SEQ = 2048
DEPTH = 2

GRID_W = 64
CTX_LEN = 256
D_A = D_MODEL
HEAD_A = 64
H_A = D_A // HEAD_A
R_W = 64
R_A = 64
R_G = 128
SHIFT_W = 3
GN_EPS = 64e-5
D_B = D_MODEL
CHUNK = 128
G_B = 8
C_B = D_B // G_B
N_EXP = 16
CAP_FACTOR = 2
D_FF = 2 * D_MODEL
N_RWKV = 3 * D_A + 2 * R_W + 2 * R_A + R_G
N_IN = N_RWKV + 2 * D_B + 2 * D_MODEL
ALPHA = (2 * DEPTH) ** 0.25
BETA = (8 * DEPTH) ** -0.25
LN_EPS = 1e-5

kernel_name = "hybrid_rwkv7_gmlp_ecmoe_diffusion_block"


def layer_norm(x, g=None, b=None, eps=LN_EPS):
    xf = x.astype(jnp.float32)
    mu = xf.mean(-1, keepdims=True)
    var = jnp.square(xf - mu).mean(-1, keepdims=True)
    y = (xf - mu) * lax.rsqrt(var + eps)
    if g is not None:
        y = y * g + b
    return y.astype(x.dtype)


def modulate(x, shift, scale):
    return layer_norm(x) * (1 + scale) + shift


def centred_shift(z, w):
    zp = jnp.pad(z, ((0, 0), (1, 1), (0, 0)))
    return zp[:, :-2] * w[0] + zp[:, 1:-1] * w[1] + zp[:, 2:] * w[2]


def rwkv_prep(zr, p):
    B, T, _ = zr.shape
    cuts = np.cumsum((D_A, D_A, D_A, R_W, R_W, R_A, R_A)).tolist()
    r, k, v, dwf, dwb, daf, dab, dg = jnp.split(zr, cuts, axis=-1)
    heads = lambda t: t.astype(jnp.float32).reshape(B, T, H_A, HEAD_A)

    def decay(dw, w0, w2):
        w = -jax.nn.softplus(-(w0 + jnp.tanh(dw) @ w2)) - 0.5
        return jnp.exp(-jnp.exp(heads(w)))

    d_f = decay(dwf, p["w0"][0], p["w2"][0])
    d_b = decay(dwb, p["w0"][1], p["w2"][1])
    a_f = jax.nn.sigmoid(p["a0"][0] + daf @ p["a2"][0])
    a_b = jax.nn.sigmoid(p["a0"][1] + dab @ p["a2"][1])
    kk = heads(k * p["k_k"])
    kk = kk * lax.rsqrt(jnp.maximum(jnp.sum(kk * kk, -1, keepdims=True), 1e-12))
    k_f = heads(k * (1 + (a_f - 1) * p["k_a"]))
    k_b = heads(k * (1 + (a_b - 1) * p["k_a"]))
    return heads(r), k_f, k_b, heads(v), d_f, d_b, kk, heads(a_f), heads(a_b), dg


def rwkv_scan(r, k, v, d, kk, a, s0, reverse):
    xs = tuple(jnp.moveaxis(t, 1, 0) for t in (r, k, v, d, kk, a))

    def step(S, inp):
        r_t, k_t, v_t, d_t, kk_t, a_t = inp
        s_kk = jnp.einsum('bhvk,bhk->bhv', S, kk_t)
        S_new = (S * d_t[:, :, None, :]
                 - s_kk[..., None] * (kk_t * a_t)[:, :, None, :]
                 + v_t[..., None] * k_t[:, :, None, :])
        y = jnp.einsum('bhvk,bhk->bhv', S if reverse else S_new, r_t)
        return S_new, y

    s_fin, ys = lax.scan(step, s0, xs, reverse=reverse)
    return jnp.moveaxis(ys, 0, 1), s_fin


def rwkv_readout(y, r, k, v, g, p):
    B, T = y.shape[:2]
    mu = y.mean(-1, keepdims=True)
    var = jnp.square(y - mu).mean(-1, keepdims=True)
    y = ((y - mu) * lax.rsqrt(var + GN_EPS)).reshape(B, T, D_A) * p["lnx_g"] + p["lnx_b"]
    bonus = (jnp.sum(r * k * p["r_k"], -1, keepdims=True) * v).reshape(B, T, D_A)
    return ((y + bonus) * g).astype(g.dtype)


def spatial_gating(z_sgu, p, n_chunks):
    B, T, _ = z_sgu.shape
    u, v = jnp.split(jax.nn.gelu(z_sgu, approximate=False), 2, axis=-1)
    v = layer_norm(v, p["sgu_ln_g"], p["sgu_ln_b"]).reshape(B, n_chunks, CHUNK, G_B, C_B)
    v = jnp.einsum('gpq,bnqgc->bnpgc', p["sgu_w"], v) + p["sgu_b"].T[:, :, None]
    return u * v.reshape(B, T, D_B)


def token_mixer(h, p, s0_f, s0_b, n_chunks):
    z = h @ p["w_in"]
    z_rwkv, z_sgu, z_gate = jnp.split(z, [N_RWKV, N_RWKV + 2 * D_B], axis=-1)
    r, k_f, k_b, v, d_f, d_b, kk, a_f, a_b, dg = rwkv_prep(centred_shift(z_rwkv, p["shift_conv"]), p)
    y_f, s_f = rwkv_scan(r, k_f, v, d_f, kk, a_f, s0_f, reverse=False)
    y_b, s_b = rwkv_scan(r, k_b, v, d_b, kk, a_b, s0_b, reverse=True)
    y_a = rwkv_readout(y_f + y_b, r, k_f, v, jax.nn.sigmoid(dg) @ p["g2"], p)
    y_s = spatial_gating(z_sgu, p, n_chunks)
    g_a, g_s = jnp.split(jax.nn.sigmoid(z_gate), 2, axis=-1)
    merged = g_a * (y_a @ p["w_branch_a"]) + g_s * (y_s @ p["w_branch_b"])
    return merged @ p["w_out"], s_f, s_b


def context_scan_states(h, p, s0):
    z_rwkv = centred_shift(h @ p["w_in"][:, :N_RWKV], p["shift_conv"])
    r, k_f, k_b, v, d_f, d_b, kk, a_f, a_b, _ = rwkv_prep(z_rwkv, p)
    _, s_f = rwkv_scan(r, k_f, v, d_f, kk, a_f, s0, reverse=False)
    _, s_b = rwkv_scan(r, k_b, v, d_b, kk, a_b, s0, reverse=True)
    return s_f, s_b


def expert_choice_ffn(h, p):
    B, T, D = h.shape
    cap = CAP_FACTOR * T // N_EXP
    aff = jax.nn.softmax((h @ p["router_w"]).astype(jnp.float32), axis=-1)
    gate, idx = lax.top_k(jnp.swapaxes(aff, 1, 2), cap)
    xe = jax.vmap(lambda hb, ib: hb[ib])(h, idx)
    hid = jax.nn.silu(jnp.einsum('becd,edf->becf', xe, p["exp_w1"])) * \
        jnp.einsum('becd,edf->becf', xe, p["exp_w3"])
    ye = jnp.einsum('becf,efd->becd', hid, p["exp_w2"]) * gate[..., None].astype(h.dtype)
    return jax.vmap(lambda yb, ib: jax.ops.segment_sum(
        yb.reshape(-1, D), ib.reshape(-1), num_segments=T))(ye, idx)


def setup_inputs(seed: int = 0) -> dict:
    key = jax.random.key(seed)
    ks = iter(jax.random.split(key, 40))
    L, D = DEPTH, D_MODEL
    nrm = lambda shape, scale: scale * jax.random.normal(next(ks), shape, jnp.float32)
    return {
        "x": nrm((BATCH, SEQ, D), 1.0),
        "c": nrm((BATCH, D), 1.0),
        "ctx": nrm((BATCH, CTX_LEN, D), 1.0),
        "c_ctx": nrm((D,), 1.0),
        "ada_w": nrm((L, D, 6 * D), D ** -0.5),
        "ada_b": nrm((L, 6 * D), 0.02),
        "w_in": nrm((L, D, N_IN), D ** -0.5),
        "shift_conv": jnp.array([0.25, 0.5, 0.25], jnp.float32)[None, :, None] + nrm((L, SHIFT_W, N_RWKV), 0.1),
        "w0": jax.random.uniform(next(ks), (L, 2, D_A), jnp.float32, -6.0, -1.0),
        "w2": nrm((L, 2, R_W, D_A), 0.5 * R_W ** -0.5),
        "a0": nrm((L, 2, D_A), 0.1),
        "a2": nrm((L, 2, R_A, D_A), 0.5 * R_A ** -0.5),
        "g2": nrm((L, R_G, D_A), R_G ** -0.5),
        "k_k": 0.85 + nrm((L, D_A), 0.05),
        "k_a": 1.0 + nrm((L, D_A), 0.05),
        "r_k": nrm((L, H_A, HEAD_A), 0.1),
        "lnx_g": 1.0 + nrm((L, D_A), 0.02),
        "lnx_b": nrm((L, D_A), 0.02),
        "sgu_ln_g": 1.0 + nrm((L, D_B), 0.02),
        "sgu_ln_b": nrm((L, D_B), 0.02),
        "sgu_w": nrm((L, G_B, CHUNK, CHUNK), 0.5 * CHUNK ** -0.5),
        "sgu_b": 1.0 + nrm((L, G_B, CHUNK), 0.1),
        "w_branch_a": nrm((L, D_A, D), D_A ** -0.5),
        "w_branch_b": nrm((L, D_B, D), D_B ** -0.5),
        "w_out": nrm((L, D, D), BETA * D ** -0.5),
        "ln1_g": 1.0 + nrm((L, D), 0.02),
        "ln1_b": nrm((L, D), 0.02),
        "router_w": nrm((L, D, N_EXP), D ** -0.5),
        "exp_w1": nrm((L, N_EXP, D, D_FF), D ** -0.5),
        "exp_w3": nrm((L, N_EXP, D, D_FF), D ** -0.5),
        "exp_w2": nrm((L, N_EXP, D_FF, D), BETA * D_FF ** -0.5),
        "ln2_g": 1.0 + nrm((L, D), 0.02),
        "ln2_b": nrm((L, D), 0.02),
    }


def reference(x, c, ctx, c_ctx, ada_w, ada_b, w_in, shift_conv, w0, w2, a0, a2, g2, k_k, k_a, r_k,
              lnx_g, lnx_b, sgu_ln_g, sgu_ln_b, sgu_w, sgu_b, w_branch_a, w_branch_b, w_out,
              ln1_g, ln1_b, router_w, exp_w1, exp_w3, exp_w2, ln2_g, ln2_b):
    B, T, _ = x.shape
    rows = T // GRID_W
    lat_chunks = rows * GRID_W // CHUNK
    ctx_chunks = ctx.shape[1] // CHUNK
    s0 = jnp.zeros((B, H_A, HEAD_A, HEAD_A), jnp.float32)
    for l in range(DEPTH):
        p = dict(w_in=w_in[l], shift_conv=shift_conv[l], w0=w0[l], w2=w2[l], a0=a0[l], a2=a2[l],
                 g2=g2[l], k_k=k_k[l], k_a=k_a[l], r_k=r_k[l], lnx_g=lnx_g[l], lnx_b=lnx_b[l],
                 sgu_ln_g=sgu_ln_g[l], sgu_ln_b=sgu_ln_b[l], sgu_w=sgu_w[l], sgu_b=sgu_b[l],
                 w_branch_a=w_branch_a[l], w_branch_b=w_branch_b[l], w_out=w_out[l],
                 router_w=router_w[l], exp_w1=exp_w1[l], exp_w3=exp_w3[l], exp_w2=exp_w2[l])
        mod_x = (jax.nn.silu(c) @ ada_w[l] + ada_b[l])[:, None, :]
        mod_c = (jax.nn.silu(c_ctx) @ ada_w[l] + ada_b[l])[None, None, :]
        sh1, sc1, gt1, sh2, sc2, gt2 = jnp.split(mod_x, 6, axis=-1)
        csh1, csc1, cgt1, csh2, csc2, cgt2 = jnp.split(mod_c, 6, axis=-1)

        hc = modulate(ctx, csh1, csc1)
        if l == DEPTH - 1:
            s_f, s_b = context_scan_states(hc, p, s0)
        else:
            yc, s_f, s_b = token_mixer(hc, p, s0, s0, ctx_chunks)
            ctx = layer_norm(ALPHA * ctx + cgt1 * yc, ln1_g[l], ln1_b[l])
            ctx = layer_norm(ALPHA * ctx + cgt2 * expert_choice_ffn(modulate(ctx, csh2, csc2), p),
                             ln2_g[l], ln2_b[l])

        yx, _, _ = token_mixer(modulate(x, sh1, sc1), p, s_f, s_b, lat_chunks)
        x = layer_norm(ALPHA * x + gt1 * yx, ln1_g[l], ln1_b[l])
        x = layer_norm(ALPHA * x + gt2 * expert_choice_ffn(modulate(x, sh2, sc2), p),
                       ln2_g[l], ln2_b[l])
    return x
```

```python
import functools

import jax
import jax.numpy as jnp
from jax import lax
from jax.experimental import pallas as pl
from jax.experimental.pallas import tpu as pltpu

LANES = 128
HEAD = 64
SCAN_CHUNK = 64
INV_BLOCK = 16
CAP_FACTOR = 2
LN_EPS = 1e-5
GN_EPS = 64e-5
VMEM_LIMIT = 56 * 1024 * 1024

_F32 = jnp.float32
_BF16 = jnp.bfloat16


def _cparams(sem):
    return pltpu.CompilerParams(dimension_semantics=sem, vmem_limit_bytes=VMEM_LIMIT)


def _bdot(a, b):
    return jnp.dot(a.astype(_BF16), b.astype(_BF16), preferred_element_type=_F32)


def _stack_heads(x, lane_lo):
    return jnp.concatenate([jnp.where(lane_lo, x, 0.0), jnp.where(lane_lo, 0.0, x)], axis=0)


def _unit_tri_inverse_apply(a, w, eye):
    n = a.shape[0]
    row = lax.broadcasted_iota(jnp.int32, (n, n), 0) // INV_BLOCK
    col = lax.broadcasted_iota(jnp.int32, (n, n), 1) // INV_BLOCK
    same = row == col
    ad = jnp.where(same, a, 0.0)
    ao = jnp.where(same, 0.0, a)
    x = -ad
    t = eye + x
    p = x
    for _ in range(3):
        p = _bdot(p, p)
        t = t + _bdot(t, p)
    e = _bdot(t, ao)
    e2 = _bdot(e, e)
    f = eye - e + e2 - _bdot(e, e2)
    return _bdot(f, _bdot(t, w))


def _scan_kernel(r_ref, k_ref, v_ref, ld_ref, kk_ref, be_ref, s0_ref, y_ref, s_ref, *, reverse, n_chunks):
    t_idx = pl.program_id(1)

    @pl.when(t_idx == 0)
    def _():
        s_ref[...] = s0_ref[...]

    c = SCAN_CHUNK
    n_pairs = r_ref.shape[-1] // LANES
    ri = lax.broadcasted_iota(jnp.int32, (c, c), 0)
    ci = lax.broadcasted_iota(jnp.int32, (c, c), 1)
    tri = (ci >= ri if reverse else ci <= ri).astype(_F32)
    lane_lo = lax.broadcasted_iota(jnp.int32, (c, LANES), 1) < HEAD
    r2 = lax.broadcasted_iota(jnp.int32, (2 * c, 2 * c), 0)
    c2 = lax.broadcasted_iota(jnp.int32, (2 * c, 2 * c), 1)
    eye = (r2 == c2).astype(_F32)
    strict = (c2 % c > r2 % c) if reverse else (c2 % c < r2 % c)
    read_mask = strict if reverse else (c2 % c <= r2 % c)

    def chunk_body(ic, carry):
        j = (n_chunks - 1 - ic) if reverse else ic
        rows = pl.ds(pl.multiple_of(j * c, c), c)
        c_in_all = jnp.dot(tri, ld_ref[0, rows, :], precision=lax.Precision.HIGHEST,
                           preferred_element_type=_F32)
        for p in range(n_pairs):
            lanes = slice(p * LANES, (p + 1) * LANES)
            ld = ld_ref[0, rows, lanes]
            c_in = c_in_all[:, lanes]
            c_ex = c_in - ld
            tot = c_in[0:1, :] if reverse else c_in[c - 1:c, :]
            e_in = jnp.exp(c_in)
            e_ex = jnp.exp(c_ex)
            e_ninv = jnp.exp(-c_in)
            e_tot = jnp.exp(tot - c_in)
            kk = kk_ref[0, rows, lanes]
            be = be_ref[0, rows, lanes]
            kx = k_ref[0, rows, lanes]
            rx = r_ref[0, rows, lanes]
            vx = v_ref[0, rows, lanes]
            lhs = jnp.concatenate([_stack_heads(kk * e_ex, lane_lo),
                                   _stack_heads(rx * (e_ex if reverse else e_in), lane_lo)], axis=0)
            rhs = jnp.concatenate([_stack_heads(kx * e_ninv, lane_lo),
                                   _stack_heads(be * e_ninv, lane_lo)], axis=0)
            aa = lax.dot_general(lhs.astype(_BF16), rhs.astype(_BF16), (((1,), (1,)), ((), ())),
                                 preferred_element_type=_F32)
            a_kk = jnp.where(strict, aa[:2 * c, :2 * c], 0.0)
            a_bk = jnp.where(strict, aa[:2 * c, 2 * c:], 0.0)
            a_rk = jnp.where(read_mask, aa[2 * c:, :2 * c], 0.0)
            a_rb = jnp.where(read_mask, aa[2 * c:, 2 * c:], 0.0)
            s = s_ref[0, p]
            sv = _bdot(lhs, s)
            vm = _stack_heads(vx, lane_lo)
            w = sv[:2 * c] + _bdot(a_kk, vm)
            u = _unit_tri_inverse_apply(a_bk, w, eye)
            vu = jnp.concatenate([vm, u], axis=0)
            yst = sv[2 * c:] + _bdot(jnp.concatenate([a_rk, -a_rb], axis=1), vu)
            y_ref[0, rows, lanes] = yst[:c] + yst[c:]
            khat = jnp.concatenate([_stack_heads(kx * e_tot, lane_lo),
                                    _stack_heads(-(be * e_tot), lane_lo)], axis=0)
            upd = lax.dot_general(khat.astype(_BF16), vu.astype(_BF16), (((0,), (0,)), ((), ())),
                                  preferred_element_type=_F32)
            dec = jnp.sum(jnp.where(eye > 0, jnp.broadcast_to(jnp.exp(tot), (2 * c, LANES)), 0.0),
                          axis=1, keepdims=True)
            s_ref[0, p] = s * dec + upd
        return carry

    lax.fori_loop(0, n_chunks, chunk_body, 0)


def _rwkv_scan(r, k, v, logd, kk, beta, s0, *, reverse, block_t):
    b, t, d = r.shape
    assert t % block_t == 0 and block_t % SCAN_CHUNK == 0 and d % LANES == 0 and 2 * HEAD == LANES
    nt = t // block_t
    tmap = (lambda bi, ti: (bi, nt - 1 - ti, 0)) if reverse else (lambda bi, ti: (bi, ti, 0))
    tok = pl.BlockSpec((1, block_t, d), tmap)
    st = pl.BlockSpec((1, d // LANES, LANES, LANES), lambda bi, ti: (bi, 0, 0, 0))
    return pl.pallas_call(
        functools.partial(_scan_kernel, reverse=reverse, n_chunks=block_t // SCAN_CHUNK),
        grid=(b, nt),
        in_specs=[tok] * 6 + [st],
        out_specs=[tok, st],
        out_shape=[jax.ShapeDtypeStruct((b, t, d), _F32),
                   jax.ShapeDtypeStruct((b, d // LANES, LANES, LANES), _F32)],
        compiler_params=_cparams(("parallel", "arbitrary")),
        name="rwkv_scan_rev" if reverse else "rwkv_scan_fwd",
    )(r, k, v, logd, kk, beta, s0)


def _layer_norm(x, eps=LN_EPS):
    mu = jnp.mean(x, axis=-1, keepdims=True)
    xc = x - mu
    var = jnp.mean(xc * xc, axis=-1, keepdims=True)
    return xc * lax.rsqrt(var + eps)


def _sigmoid(x):
    return 1.0 / (1.0 + jnp.exp(-x))


def _head_sum(x):
    n = x.shape[-1]
    lane = lax.broadcasted_iota(jnp.int32, x.shape, x.ndim - 1)
    s = 1
    while s < HEAD:
        partner = jnp.where((lane & s) == 0, pltpu.roll(x, n - s, x.ndim - 1), pltpu.roll(x, s, x.ndim - 1))
        x = x + partner
        s *= 2
    return x


def _mod_map(n_mod):
    return (lambda bi, ti: (bi, 0, 0)) if n_mod > 1 else (lambda bi, ti: (0, 0, 0))


def _vec_spec(d):
    return pl.BlockSpec((1, d), lambda bi, ti: (0, 0))


def _full_spec(shape):
    return pl.BlockSpec(shape, lambda bi, ti: (0,) * len(shape))


def _tok_spec(tm, d):
    return pl.BlockSpec((1, tm, d), lambda bi, ti: (bi, ti, 0))


def _ada_kernel(c_ref, w_ref, b_ref, o_ref):
    c = c_ref[...]
    o_ref[0] = _bdot(c * _sigmoid(c), w_ref[0]) + b_ref[0]


def _ada_modulation(cc, ada_w, ada_b, *, block_n):
    depth, d, n = ada_w.shape
    rows = cc.shape[0]
    return pl.pallas_call(
        _ada_kernel,
        grid=(depth, n // block_n),
        in_specs=[pl.BlockSpec((rows, d), lambda li, ni: (0, 0)),
                  pl.BlockSpec((1, d, block_n), lambda li, ni: (li, 0, ni)),
                  pl.BlockSpec((1, 1, block_n), lambda li, ni: (li, 0, ni))],
        out_specs=pl.BlockSpec((1, rows, block_n), lambda li, ni: (li, 0, ni)),
        out_shape=jax.ShapeDtypeStruct((depth, rows, n), _F32),
        compiler_params=_cparams(("parallel", "parallel")),
        name="ada_modulation",
    )(cc, ada_w, ada_b.reshape(depth, 1, n))


def _ln_mod_matmul_kernel(x_ref, sh_ref, sc_ref, w_ref, o_ref):
    h = _layer_norm(x_ref[0]) * (1.0 + sc_ref[0]) + sh_ref[0]
    o_ref[0] = _bdot(h, w_ref[...])


def _ln_mod_matmul(x, shift, scale, w, *, block_t):
    b, t, d = x.shape
    n = w.shape[1]
    mod = pl.BlockSpec((1, 1, d), _mod_map(shift.shape[0]))
    return pl.pallas_call(
        _ln_mod_matmul_kernel,
        grid=(b, t // block_t),
        in_specs=[_tok_spec(block_t, d), mod, mod, _full_spec((d, n))],
        out_specs=_tok_spec(block_t, n),
        out_shape=jax.ShapeDtypeStruct((b, t, n), _F32),
        compiler_params=_cparams(("parallel", "parallel")),
        name="ln_mod_matmul",
    )(x, shift, scale, w)


def _prep_kernel(z_ref, zp_ref, zn_ref, conv_ref, w0_ref, w2f_ref, w2b_ref, a0_ref, a2f_ref, a2b_ref,
                 g2_ref, kk_ref, ka_ref,
                 r_o, v_o, kkn_o, kf_o, kb_o, ldf_o, ldb_o, bf_o, bb_o, g_o):
    z = z_ref[0]
    tm = z.shape[0]
    d = r_o.shape[-1]
    row = lax.broadcasted_iota(jnp.int32, z.shape, 0)
    up = jnp.where(row == 0, zp_ref[0, 0], pltpu.roll(z, 1, 0))
    dn = jnp.where(row == tm - 1, zn_ref[0, 0], pltpu.roll(z, tm - 1, 0))
    conv = conv_ref[...]
    zs = up * conv[0:1] + z * conv[1:2] + dn * conv[2:3]
    r = zs[:, 0:d]
    k = zs[:, d:2 * d]
    v = zs[:, 2 * d:3 * d]
    tw = jnp.tanh(zs[:, 3 * d:3 * d + LANES])
    la = zs[:, 3 * d + LANES:3 * d + 2 * LANES]
    dg = zs[:, 3 * d + 2 * LANES:3 * d + 3 * LANES]

    def log_decay(w0, w2_ref):
        x = -(w0 + _bdot(tw, w2_ref[...]))
        softplus = jnp.maximum(x, 0.0) + jnp.log1p(jnp.exp(-jnp.abs(x)))
        return -jnp.exp(-softplus - 0.5)

    w0 = w0_ref[...]
    a0 = a0_ref[...]
    a_f = _sigmoid(a0[0:1] + _bdot(la, a2f_ref[...]))
    a_b = _sigmoid(a0[1:2] + _bdot(la, a2b_ref[...]))
    kkr = k * kk_ref[...]
    kkn = kkr * lax.rsqrt(jnp.maximum(_head_sum(kkr * kkr), 1e-12))
    ka = ka_ref[...]
    r_o[0] = r
    v_o[0] = v
    kkn_o[0] = kkn
    kf_o[0] = k * (1.0 + (a_f - 1.0) * ka)
    kb_o[0] = k * (1.0 + (a_b - 1.0) * ka)
    ldf_o[0] = log_decay(w0[0:1], w2f_ref)
    ldb_o[0] = log_decay(w0[1:2], w2b_ref)
    bf_o[0] = kkn * a_f
    bb_o[0] = kkn * a_b
    g_o[0] = _bdot(_sigmoid(dg), g2_ref[...])


def _rwkv_prep(z, conv, w0, w2f, w2b, a0, a2f, a2b, g2, k_k, k_a, *, d, block_t):
    b, t, nr = z.shape
    nb = t // block_t
    zero = jnp.zeros((b, 1, nr), z.dtype)
    z_prev = jnp.concatenate([zero, z[:, block_t - 1:t - 1:block_t]], axis=1).reshape(b, nb, 1, nr)
    z_next = jnp.concatenate([z[:, block_t::block_t], zero], axis=1).reshape(b, nb, 1, nr)
    halo = pl.BlockSpec((1, 1, 1, nr), lambda bi, ti: (bi, ti, 0, 0))
    out = jax.ShapeDtypeStruct((b, t, d), _F32)
    return pl.pallas_call(
        _prep_kernel,
        grid=(b, nb),
        in_specs=[_tok_spec(block_t, nr), halo, halo, _full_spec(conv.shape), _full_spec(w0.shape),
                  _full_spec(w2f.shape), _full_spec(w2b.shape), _full_spec(a0.shape), _full_spec(a2f.shape),
                  _full_spec(a2b.shape), _full_spec(g2.shape), _vec_spec(d), _vec_spec(d)],
        out_specs=[_tok_spec(block_t, d)] * 10,
        out_shape=[out] * 10,
        compiler_params=_cparams(("parallel", "parallel")),
        name="rwkv_prep",
    )(z, z_prev, z_next, conv, w0, w2f, w2b, a0, a2f, a2b, g2, k_k, k_a)


def _sgu_kernel(z_ref, lg_ref, lb_ref, w_ref, bias_ref, o_ref, *, chunk):
    z = z_ref[0]
    ge = 0.5 * z * (1.0 + lax.erf(z * (2.0 ** -0.5)))
    d = o_ref.shape[-1]
    u = ge[:, :d]
    v = _layer_norm(ge[:, d:]) * lg_ref[...] + lb_ref[...]
    bias = bias_ref[...]
    for c in range(z.shape[0] // chunk):
        rows = slice(c * chunk, (c + 1) * chunk)
        for g in range(w_ref.shape[0]):
            cols = slice(g * (d // w_ref.shape[0]), (g + 1) * (d // w_ref.shape[0]))
            mixed = _bdot(w_ref[g], v[rows, cols]) + bias[:, cols]
            o_ref[0, rows, cols] = u[rows, cols] * mixed


def _spatial_gating(z_sgu, ln_g, ln_b, w, bias_full, *, block_t):
    b, t, d2 = z_sgu.shape
    d = d2 // 2
    chunk = w.shape[-1]
    return pl.pallas_call(
        functools.partial(_sgu_kernel, chunk=chunk),
        grid=(b, t // block_t),
        in_specs=[_tok_spec(block_t, d2), _vec_spec(d), _vec_spec(d), _full_spec(w.shape),
                  _full_spec(bias_full.shape)],
        out_specs=_tok_spec(block_t, d),
        out_shape=jax.ShapeDtypeStruct((b, t, d), _F32),
        compiler_params=_cparams(("parallel", "parallel")),
        name="spatial_gating",
    )(z_sgu, ln_g, ln_b, w, bias_full)


def _merge_kernel(yf_ref, yb_ref, r_ref, kf_ref, v_ref, g_ref, ys_ref, zg_ref, x_ref, gt_ref,
                  lnxg_ref, lnxb_ref, rk_ref, wa_ref, wb_ref, wo_ref, l1g_ref, l1b_ref, o_ref, *, alpha):
    d = o_ref.shape[-1]
    y = yf_ref[0] + yb_ref[0]
    yc = y - _head_sum(y) * (1.0 / HEAD)
    var = _head_sum(yc * yc) * (1.0 / HEAD)
    yn = yc * lax.rsqrt(var + GN_EPS) * lnxg_ref[...] + lnxb_ref[...]
    bonus = _head_sum(r_ref[0] * kf_ref[0] * rk_ref[...]) * v_ref[0]
    y_a = (yn + bonus) * g_ref[0]
    zg = zg_ref[0]
    merged = _sigmoid(zg[:, :d]) * _bdot(y_a, wa_ref[...]) + _sigmoid(zg[:, d:]) * _bdot(ys_ref[0], wb_ref[...])
    out = _bdot(merged, wo_ref[...])
    o_ref[0] = _layer_norm(alpha * x_ref[0] + gt_ref[0] * out) * l1g_ref[...] + l1b_ref[...]


def _merge(yf, yb, r, kf, v, g, ys, zg, x, gate, lnx_g, lnx_b, r_k, wa, wb, wo, l1g, l1b, *, alpha, block_t):
    b, t, d = x.shape
    tok = _tok_spec(block_t, d)
    vec = _vec_spec(d)
    mat = _full_spec((d, d))
    return pl.pallas_call(
        functools.partial(_merge_kernel, alpha=alpha),
        grid=(b, t // block_t),
        in_specs=[tok] * 7 + [_tok_spec(block_t, 2 * d), tok, pl.BlockSpec((1, 1, d), _mod_map(gate.shape[0])),
                  vec, vec, vec, mat, mat, mat, vec, vec],
        out_specs=tok,
        out_shape=jax.ShapeDtypeStruct((b, t, d), _F32),
        compiler_params=_cparams(("parallel", "parallel")),
        name="branch_merge",
    )(yf, yb, r, kf, v, g, ys, zg, x, gate, lnx_g, lnx_b, r_k, wa, wb, wo, l1g, l1b)


def _router_kernel(x_ref, sh_ref, sc_ref, rw_ref, h_ref, aff_ref, *, n_exp):
    h = _layer_norm(x_ref[0]) * (1.0 + sc_ref[0]) + sh_ref[0]
    h_ref[0] = h.astype(_BF16)
    logits = jnp.dot(h, rw_ref[...], precision=lax.Precision.HIGHEST, preferred_element_type=_F32)
    lane = lax.broadcasted_iota(jnp.int32, logits.shape, 1)
    logits = jnp.where(lane < n_exp, logits, -jnp.inf)
    e = jnp.exp(logits - jnp.max(logits, axis=-1, keepdims=True))
    aff_ref[0] = e / jnp.sum(e, axis=-1, keepdims=True)


def _router(x, shift, scale, rw_pad, *, n_exp, block_t):
    b, t, d = x.shape
    mod = pl.BlockSpec((1, 1, d), _mod_map(shift.shape[0]))
    return pl.pallas_call(
        functools.partial(_router_kernel, n_exp=n_exp),
        grid=(b, t // block_t),
        in_specs=[_tok_spec(block_t, d), mod, mod, _full_spec(rw_pad.shape)],
        out_specs=[_tok_spec(block_t, d), _tok_spec(block_t, LANES)],
        out_shape=[jax.ShapeDtypeStruct((b, t, d), _BF16), jax.ShapeDtypeStruct((b, t, LANES), _F32)],
        compiler_params=_cparams(("parallel", "parallel")),
        name="moe_router",
    )(x, shift, scale, rw_pad)


def _moe_kernel(h_ref, arow_ref, acol_ref, w1_ref, w3_ref, w2_ref, o_ref, xe_ref, p_ref, ye_ref, *, rank_block,
                scatter_block):
    e_idx = pl.program_id(1)
    f_idx = pl.program_id(2)
    cap, t = p_ref.shape

    @pl.when(jnp.logical_and(e_idx == 0, f_idx == 0))
    def _():
        o_ref[...] = jnp.zeros_like(o_ref)

    @pl.when(f_idx == 0)
    def _():
        a_row = arow_ref[0, 0]
        tok_lane = lax.broadcasted_iota(jnp.int32, (rank_block, t), 1)
        tok_sub = lax.broadcasted_iota(jnp.int32, (rank_block, t), 0)
        rank = jnp.zeros((1, t), _F32)
        for j0 in range(0, t, rank_block):
            a_col = acol_ref[0, 0, j0:j0 + rank_block, :]
            beats = jnp.logical_or(a_col > a_row, jnp.logical_and(a_col == a_row, tok_sub + j0 < tok_lane))
            rank = rank + jnp.sum(beats.astype(_F32), axis=0, keepdims=True)
        slot = lax.broadcasted_iota(jnp.int32, (cap, t), 0).astype(_F32)
        p = (rank == slot).astype(_BF16)
        p_ref[...] = p
        xe_ref[...] = jnp.dot(p, h_ref[0], preferred_element_type=_F32).astype(_BF16)
        ye_ref[...] = jnp.zeros_like(ye_ref)

    xe = xe_ref[...]
    h1 = jnp.dot(xe, w1_ref[0], preferred_element_type=_F32)
    h3 = jnp.dot(xe, w3_ref[0], preferred_element_type=_F32)
    ye_ref[...] += _bdot(h1 * _sigmoid(h1) * h3, w2_ref[0])

    @pl.when(f_idx == pl.num_programs(2) - 1)
    def _():
        ye = ye_ref[...].astype(_BF16)
        for t0 in range(0, t, scatter_block):
            back = lax.dot_general(p_ref[:, t0:t0 + scatter_block], ye, (((0,), (0,)), ((), ())),
                                   preferred_element_type=_F32)
            o_ref[0, t0:t0 + scatter_block, :] += acol_ref[0, 0, t0:t0 + scatter_block, :] * back


def _expert_choice_ffn(h, aff_row, aff_col, w1, w3, w2, *, block_f):
    b, t, d = h.shape
    n_exp, _, d_ff = w1.shape
    cap = CAP_FACTOR * t // n_exp
    blk = min(t, 256)
    return pl.pallas_call(
        functools.partial(_moe_kernel, rank_block=blk, scatter_block=min(t, 512)),
        grid=(b, n_exp, d_ff // block_f),
        in_specs=[pl.BlockSpec((1, t, d), lambda bi, ei, fi: (bi, 0, 0)),
                  pl.BlockSpec((1, 1, 1, t), lambda bi, ei, fi: (bi, ei, 0, 0)),
                  pl.BlockSpec((1, 1, t, 1), lambda bi, ei, fi: (bi, ei, 0, 0)),
                  pl.BlockSpec((1, d, block_f), lambda bi, ei, fi: (ei, 0, fi)),
                  pl.BlockSpec((1, d, block_f), lambda bi, ei, fi: (ei, 0, fi)),
                  pl.BlockSpec((1, block_f, d), lambda bi, ei, fi: (ei, fi, 0))],
        out_specs=pl.BlockSpec((1, t, d), lambda bi, ei, fi: (bi, 0, 0)),
        out_shape=jax.ShapeDtypeStruct((b, t, d), _F32),
        scratch_shapes=[pltpu.VMEM((cap, d), _BF16), pltpu.VMEM((cap, t), _BF16), pltpu.VMEM((cap, d), _F32)],
        compiler_params=_cparams(("parallel", "arbitrary", "arbitrary")),
        name="expert_choice_ffn",
    )(h, aff_row, aff_col, w1, w3, w2)


def _post_norm_kernel(x_ref, y_ref, gt_ref, g_ref, b_ref, o_ref, *, alpha):
    o_ref[0] = _layer_norm(alpha * x_ref[0] + gt_ref[0] * y_ref[0]) * g_ref[...] + b_ref[...]


def _post_norm(x, y, gate, g, bias, *, alpha, block_t):
    b, t, d = x.shape
    tok = _tok_spec(block_t, d)
    return pl.pallas_call(
        functools.partial(_post_norm_kernel, alpha=alpha),
        grid=(b, t // block_t),
        in_specs=[tok, tok, pl.BlockSpec((1, 1, d), _mod_map(gate.shape[0])), _vec_spec(d), _vec_spec(d)],
        out_specs=tok,
        out_shape=jax.ShapeDtypeStruct((b, t, d), _F32),
        compiler_params=_cparams(("parallel", "parallel")),
        name="post_norm",
    )(x, y, gate, g, bias)


def _block_t(t, want):
    return want if t % want == 0 else t


def _pad_rows(w, top):
    out = jnp.zeros((LANES, w.shape[1]), w.dtype)
    return lax.dynamic_update_slice(out, w, (top, 0))


def kernel(x, c, ctx, c_ctx, ada_w, ada_b, w_in, shift_conv, w0, w2, a0, a2, g2, k_k, k_a, r_k, lnx_g, lnx_b,
           sgu_ln_g, sgu_ln_b, sgu_w, sgu_b, w_branch_a, w_branch_b, w_out, ln1_g, ln1_b, router_w, exp_w1,
           exp_w3, exp_w2, ln2_g, ln2_b):
    bsz, seq, d = x.shape
    depth = w_in.shape[0]
    n_rwkv = shift_conv.shape[-1]
    r_w, r_a, r_g = w2.shape[2], a2.shape[2], g2.shape[1]
    n_exp = router_w.shape[-1]
    assert 2 * r_w == LANES and 2 * r_a == LANES and r_g == LANES and n_rwkv == 3 * d + 3 * LANES
    assert r_k.shape[2] == HEAD and d % LANES == 0
    alpha = (2 * depth) ** 0.25

    n_rows = -(-(bsz + 1) // 8) * 8
    cc = jnp.zeros((n_rows, d), _F32).at[:bsz].set(c).at[bsz].set(c_ctx)
    mods = _ada_modulation(cc, ada_w, ada_b, block_n=d)

    s_zero = jnp.zeros((bsz, d // LANES, LANES, LANES), _F32)
    vec = lambda p: p.reshape(1, d)

    for l in range(depth):
        mod_x = mods[l, :bsz].reshape(bsz, 1, 6, d)
        mod_c = mods[l, bsz].reshape(1, 1, 6, d)
        lat_mod = [mod_x[:, :, i] for i in range(6)]
        ctx_mod = [mod_c[:, :, i] for i in range(6)]
        w_rwkv = w_in[l, :, :n_rwkv].astype(_BF16)
        w_sgu = w_in[l, :, n_rwkv:n_rwkv + 2 * d].astype(_BF16)
        w_gate = w_in[l, :, n_rwkv + 2 * d:].astype(_BF16)
        prep_w = dict(conv=shift_conv[l], w0=w0[l],
                      w2f=_pad_rows(w2[l, 0], 0).astype(_BF16), w2b=_pad_rows(w2[l, 1], r_w).astype(_BF16),
                      a0=a0[l], a2f=_pad_rows(a2[l, 0], 0).astype(_BF16),
                      a2b=_pad_rows(a2[l, 1], r_a).astype(_BF16), g2=g2[l].astype(_BF16),
                      k_k=vec(k_k[l]), k_a=vec(k_a[l]))
        chunk = sgu_w.shape[-1]
        sgu_bias = jnp.repeat(sgu_b[l].T, d // sgu_w.shape[1], axis=1)
        mix_w = (vec(lnx_g[l]), vec(lnx_b[l]), r_k[l].reshape(1, d), w_branch_a[l].astype(_BF16),
                 w_branch_b[l].astype(_BF16), w_out[l].astype(_BF16), vec(ln1_g[l]), vec(ln1_b[l]))
        rw_pad = jnp.zeros((d, LANES), _F32).at[:, :n_exp].set(router_w[l])
        ew1, ew3, ew2 = exp_w1[l].astype(_BF16), exp_w3[l].astype(_BF16), exp_w2[l].astype(_BF16)

        def mixer(tok, mod, s0_f, s0_b, full):
            t = tok.shape[1]
            bt = _block_t(t, 512)
            z_r = _ln_mod_matmul(tok, mod[0], mod[1], w_rwkv, block_t=bt)
            r, v, kk, kf, kb, ldf, ldb, bef, beb, g = _rwkv_prep(z_r, d=d, block_t=_block_t(t, 256), **prep_w)
            y_f, s_f = _rwkv_scan(r, kf, v, ldf, kk, bef, s0_f, reverse=False, block_t=_block_t(t, 256))
            y_b, s_b = _rwkv_scan(r, kb, v, ldb, kk, beb, s0_b, reverse=True, block_t=_block_t(t, 256))
            if not full:
                return None, s_f, s_b
            z_s = _ln_mod_matmul(tok, mod[0], mod[1], w_sgu, block_t=bt)
            z_g = _ln_mod_matmul(tok, mod[0], mod[1], w_gate, block_t=bt)
            y_s = _spatial_gating(z_s, vec(sgu_ln_g[l]), vec(sgu_ln_b[l]), sgu_w[l].astype(_BF16), sgu_bias,
                                  block_t=_block_t(t, 2 * chunk))
            out = _merge(y_f, y_b, r, kf, v, g, y_s, z_g, tok, mod[2], *mix_w, alpha=alpha,
                         block_t=_block_t(t, 256))
            return out, s_f, s_b

        def moe(tok, mod):
            t = tok.shape[1]
            h, aff = _router(tok, mod[3], mod[4], rw_pad, n_exp=n_exp, block_t=_block_t(t, 512))
            aff_t = jnp.swapaxes(aff[:, :, :n_exp], 1, 2)
            y = _expert_choice_ffn(h, aff_t[:, :, None, :], aff_t[:, :, :, None], ew1, ew3, ew2,
                                   block_f=_block_t(ew1.shape[-1], 1024))
            return _post_norm(tok, y, mod[5], vec(ln2_g[l]), vec(ln2_b[l]), alpha=alpha, block_t=_block_t(t, 512))

        if l == depth - 1:
            _, s_f, s_b = mixer(ctx, ctx_mod, s_zero, s_zero, full=False)
        else:
            ctx, s_f, s_b = mixer(ctx, ctx_mod, s_zero, s_zero, full=True)
            ctx = moe(ctx, ctx_mod)
        x, _, _ = mixer(x, lat_mod, s_f, s_b, full=True)
        x = moe(x, lat_mod)
    return x
```

```python
import functools

import jax
import jax.numpy as jnp
from jax import lax
from jax.experimental import pallas as pl
from jax.experimental.pallas import tpu as pltpu

LANES = 128
SUBLANES = 8
HEAD = 64
SCAN_CHUNK = 64
INV_BLOCK = 16
CAP_FACTOR = 2
LN_EPS = 1e-5
GN_EPS = 64e-5
VMEM_LIMIT = 56 * 1024 * 1024

_F32 = jnp.float32
_BF16 = jnp.bfloat16


def _cparams(sem):
    return pltpu.CompilerParams(dimension_semantics=sem, vmem_limit_bytes=VMEM_LIMIT)


def _bdot(a, b):
    return jnp.dot(a.astype(_BF16), b.astype(_BF16), preferred_element_type=_F32)


def _stack_heads(x, lane_lo):
    return jnp.concatenate([jnp.where(lane_lo, x, 0.0), jnp.where(lane_lo, 0.0, x)], axis=0)


def _unit_tri_inverses(mats, eye):
    n = eye.shape[0]
    row = lax.broadcasted_iota(jnp.int32, (n, n), 0) // INV_BLOCK
    col = lax.broadcasted_iota(jnp.int32, (n, n), 1) // INV_BLOCK
    same = row == col
    p = [jnp.where(same, -a, 0.0) for a in mats]
    t = [eye + x for x in p]
    for _ in range(3):
        p = [_bdot(x, x) for x in p]
        t = [ti + _bdot(ti, x) for ti, x in zip(t, p)]
    e = [_bdot(ti, jnp.where(same, 0.0, a)) for ti, a in zip(t, mats)]
    e2 = [_bdot(x, x) for x in e]
    f = [eye - x + x2 - _bdot(x, x2) for x, x2 in zip(e, e2)]
    return [_bdot(fi, ti) for fi, ti in zip(f, t)]


def _scan_kernel(r_ref, k_ref, v_ref, ld_ref, kk_ref, be_ref, s0_ref, y_ref, s_ref,
                 q_scr, y0_scr, m_scr, n0_scr, dec_scr, *, reverse, n_chunks):
    t_idx = pl.program_id(1)

    @pl.when(t_idx == 0)
    def _():
        s_ref[...] = s0_ref[...]

    c = SCAN_CHUNK
    pairs = range(r_ref.shape[-1] // LANES)
    lanes = [slice(p * LANES, (p + 1) * LANES) for p in pairs]
    ri = lax.broadcasted_iota(jnp.int32, (c, c), 0)
    ci = lax.broadcasted_iota(jnp.int32, (c, c), 1)
    tri = (ci >= ri if reverse else ci <= ri).astype(_F32)
    lane_lo = lax.broadcasted_iota(jnp.int32, (c, LANES), 1) < HEAD
    r2 = lax.broadcasted_iota(jnp.int32, (2 * c, 2 * c), 0)
    c2 = lax.broadcasted_iota(jnp.int32, (2 * c, 2 * c), 1)
    eye = (r2 == c2).astype(_F32)
    strict = (c2 % c > r2 % c) if reverse else (c2 % c < r2 % c)
    read_mask = strict if reverse else (c2 % c <= r2 % c)
    stack = lambda x: _stack_heads(x, lane_lo)
    contract_rows = (((0,), (0,)), ((), ()))

    def chunk_terms(j, carry):
        rows = pl.ds(pl.multiple_of(j * c, c), c)
        c_in_all = jnp.dot(tri, ld_ref[0, rows, :], precision=lax.Precision.HIGHEST,
                           preferred_element_type=_F32)
        c_in = [c_in_all[:, s] for s in lanes]
        tot = [x[0:1, :] if reverse else x[c - 1:c, :] for x in c_in]
        e_ex = [jnp.exp(x - ld_ref[0, rows, s]) for x, s in zip(c_in, lanes)]
        e_rd = e_ex if reverse else [jnp.exp(x) for x in c_in]
        e_ninv = [jnp.exp(-x) for x in c_in]
        e_tot = [jnp.exp(t - x) for t, x in zip(tot, c_in)]
        kx = [k_ref[0, rows, s] for s in lanes]
        be = [be_ref[0, rows, s] for s in lanes]
        kap = [stack(kk_ref[0, rows, s] * e) for s, e in zip(lanes, e_ex)]
        rr = [stack(r_ref[0, rows, s] * e) for s, e in zip(lanes, e_rd)]
        lhs = [jnp.concatenate([a, b], axis=0).astype(_BF16) for a, b in zip(kap, rr)]
        rhs = [jnp.concatenate([stack(k * e), stack(b * e)], axis=0).astype(_BF16)
               for k, b, e in zip(kx, be, e_ninv)]
        aa = [lax.dot_general(a, b, (((1,), (1,)), ((), ())), preferred_element_type=_F32)
              for a, b in zip(lhs, rhs)]
        tf = _unit_tri_inverses([jnp.where(strict, x[:2 * c, 2 * c:], 0.0) for x in aa], eye)
        vm = [stack(v_ref[0, rows, s]).astype(_BF16) for s in lanes]
        a_v = [jnp.concatenate([jnp.where(strict, x[:2 * c, :2 * c], 0.0),
                                jnp.where(read_mask, x[2 * c:, :2 * c], 0.0)], axis=0) for x in aa]
        av = [_bdot(a, v) for a, v in zip(a_v, vm)]
        pu = [_bdot(t, jnp.concatenate([a, b[:2 * c]], axis=1)).astype(_BF16) for t, a, b in zip(tf, kap, av)]
        arb = [_bdot(jnp.where(read_mask, x[2 * c:, 2 * c:], 0.0), z) for x, z in zip(aa, pu)]
        bhat = [stack(b * e).astype(_BF16) for b, e in zip(be, e_tot)]
        khat = [stack(k * e).astype(_BF16) for k, e in zip(kx, e_tot)]
        bp = [lax.dot_general(b, z, contract_rows, preferred_element_type=_F32) for b, z in zip(bhat, pu)]
        kv = [lax.dot_general(k, v, contract_rows, preferred_element_type=_F32) for k, v in zip(khat, vm)]
        for p in pairs:
            q = rr[p] - arb[p][:, :2 * c]
            y0 = av[p][2 * c:] - arb[p][:, 2 * c:]
            q_scr[j, p] = (q[:c] + q[c:]).astype(_BF16)
            y0_scr[j, p] = y0[:c] + y0[c:]
            m_scr[j, p] = (-bp[p][:, :2 * c]).astype(_BF16)
            n0_scr[j, p] = kv[p] - bp[p][:, 2 * c:]
            dec = jnp.sum(jnp.where(eye > 0, jnp.broadcast_to(jnp.exp(tot[p]), (2 * c, LANES)), 0.0),
                          axis=1, keepdims=True)
            dec_scr[j, p] = jnp.broadcast_to(dec, (2 * c, LANES))
        return carry

    lax.fori_loop(0, n_chunks, chunk_terms, 0)

    def advance(ic, carry):
        j = (n_chunks - 1 - ic) if reverse else ic
        rows = pl.ds(pl.multiple_of(j * c, c), c)
        s = [s_ref[0, p] for p in pairs]
        sb = [x.astype(_BF16) for x in s]
        y = [jnp.dot(q_scr[j, p], sb[p], preferred_element_type=_F32) + y0_scr[j, p] for p in pairs]
        s_new = [dec_scr[j, p] * s[p] + jnp.dot(m_scr[j, p], sb[p], preferred_element_type=_F32) + n0_scr[j, p]
                 for p in pairs]
        for p in pairs:
            y_ref[0, rows, lanes[p]] = y[p]
            s_ref[0, p] = s_new[p]
        return carry

    lax.fori_loop(0, n_chunks, advance, 0)


def _rwkv_scan(r, k, v, logd, kk, beta, s0, *, reverse, block_t):
    b, t, d = r.shape
    assert t % block_t == 0 and block_t % SCAN_CHUNK == 0 and d % LANES == 0 and 2 * HEAD == LANES
    nt = t // block_t
    tmap = (lambda bi, ti: (bi, nt - 1 - ti, 0)) if reverse else (lambda bi, ti: (bi, ti, 0))
    tok = pl.BlockSpec((1, block_t, d), tmap)
    st = pl.BlockSpec((1, d // LANES, LANES, LANES), lambda bi, ti: (bi, 0, 0, 0))
    n_chunks, n_pairs = block_t // SCAN_CHUNK, d // LANES
    return pl.pallas_call(
        functools.partial(_scan_kernel, reverse=reverse, n_chunks=n_chunks),
        grid=(b, nt),
        in_specs=[tok] * 6 + [st],
        out_specs=[tok, st],
        out_shape=[jax.ShapeDtypeStruct((b, t, d), _F32),
                   jax.ShapeDtypeStruct((b, d // LANES, LANES, LANES), _F32)],
        scratch_shapes=[pltpu.VMEM((n_chunks, n_pairs, SCAN_CHUNK, LANES), _BF16),
                        pltpu.VMEM((n_chunks, n_pairs, SCAN_CHUNK, LANES), _F32),
                        pltpu.VMEM((n_chunks, n_pairs, LANES, LANES), _BF16),
                        pltpu.VMEM((n_chunks, n_pairs, LANES, LANES), _F32),
                        pltpu.VMEM((n_chunks, n_pairs, LANES, LANES), _F32)],
        compiler_params=_cparams(("parallel", "arbitrary")),
        name="rwkv_scan_rev" if reverse else "rwkv_scan_fwd",
    )(r, k, v, logd, kk, beta, s0)


def _layer_norm(x, eps=LN_EPS):
    mu = jnp.mean(x, axis=-1, keepdims=True)
    xc = x - mu
    var = jnp.mean(xc * xc, axis=-1, keepdims=True)
    return xc * lax.rsqrt(var + eps)


def _sigmoid(x):
    return 1.0 / (1.0 + jnp.exp(-x))


def _head_sum(x):
    n = x.shape[-1]
    lane = lax.broadcasted_iota(jnp.int32, x.shape, x.ndim - 1)
    s = 1
    while s < HEAD:
        partner = jnp.where((lane & s) == 0, pltpu.roll(x, n - s, x.ndim - 1), pltpu.roll(x, s, x.ndim - 1))
        x = x + partner
        s *= 2
    return x


def _mod_map(n_mod):
    return (lambda bi, ti: (bi, 0, 0)) if n_mod > 1 else (lambda bi, ti: (0, 0, 0))


def _vec_spec(d):
    return pl.BlockSpec((1, d), lambda bi, ti: (0, 0))


def _full_spec(shape):
    return pl.BlockSpec(shape, lambda bi, ti: (0,) * len(shape))


def _tok_spec(tm, d):
    return pl.BlockSpec((1, tm, d), lambda bi, ti: (bi, ti, 0))


def _ada_kernel(c_ref, w_ref, b_ref, o_ref):
    c = c_ref[...]
    o_ref[0] = _bdot(c * _sigmoid(c), w_ref[0]) + b_ref[0]


def _ada_modulation(cc, ada_w, ada_b, *, block_n):
    depth, d, n = ada_w.shape
    rows = cc.shape[0]
    return pl.pallas_call(
        _ada_kernel,
        grid=(depth, n // block_n),
        in_specs=[pl.BlockSpec((rows, d), lambda li, ni: (0, 0)),
                  pl.BlockSpec((1, d, block_n), lambda li, ni: (li, 0, ni)),
                  pl.BlockSpec((1, 1, block_n), lambda li, ni: (li, 0, ni))],
        out_specs=pl.BlockSpec((1, rows, block_n), lambda li, ni: (li, 0, ni)),
        out_shape=jax.ShapeDtypeStruct((depth, rows, n), _F32),
        compiler_params=_cparams(("parallel", "parallel")),
        name="ada_modulation",
    )(cc, ada_w, ada_b.reshape(depth, 1, n))


def _ln_mod_matmul_kernel(x_ref, sh_ref, sc_ref, w_ref, o_ref):
    h = _layer_norm(x_ref[0]) * (1.0 + sc_ref[0]) + sh_ref[0]
    o_ref[0] = _bdot(h, w_ref[...])


def _ln_mod_matmul(x, shift, scale, w, *, block_t):
    b, t, d = x.shape
    n = w.shape[1]
    mod = pl.BlockSpec((1, 1, d), _mod_map(shift.shape[0]))
    return pl.pallas_call(
        _ln_mod_matmul_kernel,
        grid=(b, t // block_t),
        in_specs=[_tok_spec(block_t, d), mod, mod, _full_spec((d, n))],
        out_specs=_tok_spec(block_t, n),
        out_shape=jax.ShapeDtypeStruct((b, t, n), _F32),
        compiler_params=_cparams(("parallel", "parallel")),
        name="ln_mod_matmul",
    )(x, shift, scale, w)


def _prep_kernel(z_ref, zp_ref, zn_ref, conv_ref, w0_ref, w2f_ref, w2b_ref, a0_ref, a2f_ref, a2b_ref,
                 g2_ref, kk_ref, ka_ref,
                 r_o, v_o, kkn_o, kf_o, kb_o, ldf_o, ldb_o, bf_o, bb_o, g_o):
    z = z_ref[0]
    tm = z.shape[0]
    d = r_o.shape[-1]
    row = lax.broadcasted_iota(jnp.int32, z.shape, 0)
    t_idx = pl.program_id(1)
    before = jnp.where(t_idx == 0, 0.0, zp_ref[0, 7:8, :])
    after = jnp.where(t_idx == pl.num_programs(1) - 1, 0.0, zn_ref[0, 0:1, :])
    up = jnp.where(row == 0, before, pltpu.roll(z, 1, 0))
    dn = jnp.where(row == tm - 1, after, pltpu.roll(z, tm - 1, 0))
    conv = conv_ref[...]
    zs = up * conv[0:1] + z * conv[1:2] + dn * conv[2:3]
    r = zs[:, 0:d]
    k = zs[:, d:2 * d]
    v = zs[:, 2 * d:3 * d]
    tw = jnp.tanh(zs[:, 3 * d:3 * d + LANES])
    la = zs[:, 3 * d + LANES:3 * d + 2 * LANES]
    dg = zs[:, 3 * d + 2 * LANES:3 * d + 3 * LANES]

    def log_decay(w0, w2_ref):
        x = -(w0 + _bdot(tw, w2_ref[...]))
        softplus = jnp.maximum(x, 0.0) + jnp.log1p(jnp.exp(-jnp.abs(x)))
        return -jnp.exp(-softplus - 0.5)

    w0 = w0_ref[...]
    a0 = a0_ref[...]
    a_f = _sigmoid(a0[0:1] + _bdot(la, a2f_ref[...]))
    a_b = _sigmoid(a0[1:2] + _bdot(la, a2b_ref[...]))
    kkr = k * kk_ref[...]
    kkn = kkr * lax.rsqrt(jnp.maximum(_head_sum(kkr * kkr), 1e-12))
    ka = ka_ref[...]
    r_o[0] = r
    v_o[0] = v
    kkn_o[0] = kkn
    kf_o[0] = k * (1.0 + (a_f - 1.0) * ka)
    kb_o[0] = k * (1.0 + (a_b - 1.0) * ka)
    ldf_o[0] = log_decay(w0[0:1], w2f_ref)
    ldb_o[0] = log_decay(w0[1:2], w2b_ref)
    bf_o[0] = kkn * a_f
    bb_o[0] = kkn * a_b
    g_o[0] = _bdot(_sigmoid(dg), g2_ref[...])


def _rwkv_prep(z, conv, w0, w2f, w2b, a0, a2f, a2b, g2, k_k, k_a, *, d, block_t):
    b, t, nr = z.shape
    nb = t // block_t
    tiles = block_t // SUBLANES
    prev_tile = pl.BlockSpec((1, SUBLANES, nr), lambda bi, ti: (bi, jnp.maximum(ti * tiles - 1, 0), 0))
    next_tile = pl.BlockSpec((1, SUBLANES, nr), lambda bi, ti: (bi, jnp.minimum((ti + 1) * tiles, nb * tiles - 1), 0))
    out = jax.ShapeDtypeStruct((b, t, d), _F32)
    return pl.pallas_call(
        _prep_kernel,
        grid=(b, nb),
        in_specs=[_tok_spec(block_t, nr), prev_tile, next_tile, _full_spec(conv.shape), _full_spec(w0.shape),
                  _full_spec(w2f.shape), _full_spec(w2b.shape), _full_spec(a0.shape), _full_spec(a2f.shape),
                  _full_spec(a2b.shape), _full_spec(g2.shape), _vec_spec(d), _vec_spec(d)],
        out_specs=[_tok_spec(block_t, d)] * 10,
        out_shape=[out] * 10,
        compiler_params=_cparams(("parallel", "parallel")),
        name="rwkv_prep",
    )(z, z, z, conv, w0, w2f, w2b, a0, a2f, a2b, g2, k_k, k_a)


def _sgu_kernel(z_ref, lg_ref, lb_ref, w_ref, bias_ref, o_ref, *, chunk):
    z = z_ref[0]
    ge = 0.5 * z * (1.0 + lax.erf(z * (2.0 ** -0.5)))
    d = o_ref.shape[-1]
    u = ge[:, :d]
    v = _layer_norm(ge[:, d:]) * lg_ref[...] + lb_ref[...]
    bias = bias_ref[...]
    for c in range(z.shape[0] // chunk):
        rows = slice(c * chunk, (c + 1) * chunk)
        for g in range(w_ref.shape[0]):
            cols = slice(g * (d // w_ref.shape[0]), (g + 1) * (d // w_ref.shape[0]))
            mixed = _bdot(w_ref[g], v[rows, cols]) + bias[:, cols]
            o_ref[0, rows, cols] = u[rows, cols] * mixed


def _spatial_gating(z_sgu, ln_g, ln_b, w, bias_full, *, block_t):
    b, t, d2 = z_sgu.shape
    d = d2 // 2
    chunk = w.shape[-1]
    return pl.pallas_call(
        functools.partial(_sgu_kernel, chunk=chunk),
        grid=(b, t // block_t),
        in_specs=[_tok_spec(block_t, d2), _vec_spec(d), _vec_spec(d), _full_spec(w.shape),
                  _full_spec(bias_full.shape)],
        out_specs=_tok_spec(block_t, d),
        out_shape=jax.ShapeDtypeStruct((b, t, d), _F32),
        compiler_params=_cparams(("parallel", "parallel")),
        name="spatial_gating",
    )(z_sgu, ln_g, ln_b, w, bias_full)


def _merge_kernel(yf_ref, yb_ref, r_ref, kf_ref, v_ref, g_ref, ys_ref, zg_ref, x_ref, gt_ref,
                  lnxg_ref, lnxb_ref, rk_ref, wa_ref, wb_ref, wo_ref, l1g_ref, l1b_ref, o_ref, *, alpha):
    d = o_ref.shape[-1]
    y = yf_ref[0] + yb_ref[0]
    yc = y - _head_sum(y) * (1.0 / HEAD)
    var = _head_sum(yc * yc) * (1.0 / HEAD)
    yn = yc * lax.rsqrt(var + GN_EPS) * lnxg_ref[...] + lnxb_ref[...]
    bonus = _head_sum(r_ref[0] * kf_ref[0] * rk_ref[...]) * v_ref[0]
    y_a = (yn + bonus) * g_ref[0]
    zg = zg_ref[0]
    merged = _sigmoid(zg[:, :d]) * _bdot(y_a, wa_ref[...]) + _sigmoid(zg[:, d:]) * _bdot(ys_ref[0], wb_ref[...])
    out = _bdot(merged, wo_ref[...])
    o_ref[0] = _layer_norm(alpha * x_ref[0] + gt_ref[0] * out) * l1g_ref[...] + l1b_ref[...]


def _merge(yf, yb, r, kf, v, g, ys, zg, x, gate, lnx_g, lnx_b, r_k, wa, wb, wo, l1g, l1b, *, alpha, block_t):
    b, t, d = x.shape
    tok = _tok_spec(block_t, d)
    vec = _vec_spec(d)
    mat = _full_spec((d, d))
    return pl.pallas_call(
        functools.partial(_merge_kernel, alpha=alpha),
        grid=(b, t // block_t),
        in_specs=[tok] * 7 + [_tok_spec(block_t, 2 * d), tok, pl.BlockSpec((1, 1, d), _mod_map(gate.shape[0])),
                  vec, vec, vec, mat, mat, mat, vec, vec],
        out_specs=tok,
        out_shape=jax.ShapeDtypeStruct((b, t, d), _F32),
        compiler_params=_cparams(("parallel", "parallel")),
        name="branch_merge",
    )(yf, yb, r, kf, v, g, ys, zg, x, gate, lnx_g, lnx_b, r_k, wa, wb, wo, l1g, l1b)


def _router_kernel(x_ref, sh_ref, sc_ref, rw_ref, h_ref, aff_ref, *, n_exp):
    h = _layer_norm(x_ref[0]) * (1.0 + sc_ref[0]) + sh_ref[0]
    h_ref[0] = h.astype(_BF16)
    logits = jnp.dot(h, rw_ref[...], precision=lax.Precision.HIGHEST, preferred_element_type=_F32)
    lane = lax.broadcasted_iota(jnp.int32, logits.shape, 1)
    logits = jnp.where(lane < n_exp, logits, -jnp.inf)
    e = jnp.exp(logits - jnp.max(logits, axis=-1, keepdims=True))
    aff_ref[0] = e / jnp.sum(e, axis=-1, keepdims=True)


def _router(x, shift, scale, rw_pad, *, n_exp, block_t):
    b, t, d = x.shape
    mod = pl.BlockSpec((1, 1, d), _mod_map(shift.shape[0]))
    return pl.pallas_call(
        functools.partial(_router_kernel, n_exp=n_exp),
        grid=(b, t // block_t),
        in_specs=[_tok_spec(block_t, d), mod, mod, _full_spec(rw_pad.shape)],
        out_specs=[_tok_spec(block_t, d), _tok_spec(block_t, LANES)],
        out_shape=[jax.ShapeDtypeStruct((b, t, d), _BF16), jax.ShapeDtypeStruct((b, t, LANES), _F32)],
        compiler_params=_cparams(("parallel", "parallel")),
        name="moe_router",
    )(x, shift, scale, rw_pad)


def _moe_kernel(*refs, n_sets):
    ins, rest = refs[:3 * n_sets], refs[3 * n_sets:]
    w1_ref, w3_ref, w2_ref = rest[:3]
    outs = rest[3:3 + n_sets]
    xe_ref, ye_ref = rest[3 + n_sets:5 + n_sets]
    p_refs = rest[5 + n_sets:]
    e_idx = pl.program_id(1)
    f_idx = pl.program_id(2)
    caps = [p.shape[0] for p in p_refs]
    offs = [sum(caps[:i]) for i in range(n_sets)]

    @pl.when(jnp.logical_and(e_idx == 0, f_idx == 0))
    def _():
        for o_ref in outs:
            o_ref[...] = jnp.zeros_like(o_ref)

    @pl.when(f_idx == 0)
    def _():
        for i in range(n_sets):
            h_ref, arow_ref, acol_ref = ins[3 * i:3 * i + 3]
            cap, t = p_refs[i].shape
            blk = min(t, 256)
            a_row = arow_ref[0, 0]
            tok_lane = lax.broadcasted_iota(jnp.int32, (blk, t), 1)
            tok_sub = lax.broadcasted_iota(jnp.int32, (blk, t), 0)
            rank = jnp.zeros((1, t), _F32)
            for j0 in range(0, t, blk):
                a_col = acol_ref[0, 0, j0:j0 + blk, :]
                beats = jnp.logical_or(a_col > a_row, jnp.logical_and(a_col == a_row, tok_sub + j0 < tok_lane))
                rank = rank + jnp.sum(beats.astype(_F32), axis=0, keepdims=True)
            slot = lax.broadcasted_iota(jnp.int32, (cap, t), 0).astype(_F32)
            p = (rank == slot).astype(_BF16)
            p_refs[i][...] = p
            xe_ref[offs[i]:offs[i] + cap, :] = jnp.dot(p, h_ref[0], preferred_element_type=_F32).astype(_BF16)
        ye_ref[...] = jnp.zeros_like(ye_ref)

    xe = xe_ref[...]
    h1 = jnp.dot(xe, w1_ref[0], preferred_element_type=_F32)
    h3 = jnp.dot(xe, w3_ref[0], preferred_element_type=_F32)
    ye_ref[...] += _bdot(h1 * _sigmoid(h1) * h3, w2_ref[0])

    @pl.when(f_idx == pl.num_programs(2) - 1)
    def _():
        for i in range(n_sets):
            acol_ref, o_ref = ins[3 * i + 2], outs[i]
            cap, t = p_refs[i].shape
            blk = min(t, 512)
            ye = ye_ref[offs[i]:offs[i] + cap, :].astype(_BF16)
            for t0 in range(0, t, blk):
                back = lax.dot_general(p_refs[i][:, t0:t0 + blk], ye, (((0,), (0,)), ((), ())),
                                       preferred_element_type=_F32)
                o_ref[0, t0:t0 + blk, :] += acol_ref[0, 0, t0:t0 + blk, :] * back


def _expert_choice_ffn(sets, w1, w3, w2, *, block_f):
    n_exp, d, d_ff = w1.shape
    b = sets[0][0].shape[0]
    caps = [CAP_FACTOR * h.shape[1] // n_exp for h, _, _ in sets]
    in_specs, out_specs, out_shape, operands = [], [], [], []
    for h, a_row, a_col in sets:
        t = h.shape[1]
        in_specs += [pl.BlockSpec((1, t, d), lambda bi, ei, fi: (bi, 0, 0)),
                     pl.BlockSpec((1, 1, 1, t), lambda bi, ei, fi: (bi, ei, 0, 0)),
                     pl.BlockSpec((1, 1, t, 1), lambda bi, ei, fi: (bi, ei, 0, 0))]
        out_specs.append(pl.BlockSpec((1, t, d), lambda bi, ei, fi: (bi, 0, 0)))
        out_shape.append(jax.ShapeDtypeStruct((b, t, d), _F32))
        operands += [h, a_row, a_col]
    in_specs += [pl.BlockSpec((1, d, block_f), lambda bi, ei, fi: (ei, 0, fi)),
                 pl.BlockSpec((1, d, block_f), lambda bi, ei, fi: (ei, 0, fi)),
                 pl.BlockSpec((1, block_f, d), lambda bi, ei, fi: (ei, fi, 0))]
    return pl.pallas_call(
        functools.partial(_moe_kernel, n_sets=len(sets)),
        grid=(b, n_exp, d_ff // block_f),
        in_specs=in_specs,
        out_specs=out_specs,
        out_shape=out_shape,
        scratch_shapes=[pltpu.VMEM((sum(caps), d), _BF16), pltpu.VMEM((sum(caps), d), _F32)]
        + [pltpu.VMEM((cap, h.shape[1]), _BF16) for cap, (h, _, _) in zip(caps, sets)],
        compiler_params=_cparams(("parallel", "arbitrary", "arbitrary")),
        name="expert_choice_ffn",
    )(*operands, w1, w3, w2)


def _post_norm_kernel(x_ref, y_ref, gt_ref, g_ref, b_ref, o_ref, *, alpha):
    o_ref[0] = _layer_norm(alpha * x_ref[0] + gt_ref[0] * y_ref[0]) * g_ref[...] + b_ref[...]


def _post_norm(x, y, gate, g, bias, *, alpha, block_t):
    b, t, d = x.shape
    tok = _tok_spec(block_t, d)
    return pl.pallas_call(
        functools.partial(_post_norm_kernel, alpha=alpha),
        grid=(b, t // block_t),
        in_specs=[tok, tok, pl.BlockSpec((1, 1, d), _mod_map(gate.shape[0])), _vec_spec(d), _vec_spec(d)],
        out_specs=tok,
        out_shape=jax.ShapeDtypeStruct((b, t, d), _F32),
        compiler_params=_cparams(("parallel", "parallel")),
        name="post_norm",
    )(x, y, gate, g, bias)


def _block_t(t, want):
    return want if t % want == 0 else t


def _pad_rows(w, top):
    out = jnp.zeros((LANES, w.shape[1]), w.dtype)
    return lax.dynamic_update_slice(out, w, (top, 0))


def kernel(x, c, ctx, c_ctx, ada_w, ada_b, w_in, shift_conv, w0, w2, a0, a2, g2, k_k, k_a, r_k, lnx_g, lnx_b,
           sgu_ln_g, sgu_ln_b, sgu_w, sgu_b, w_branch_a, w_branch_b, w_out, ln1_g, ln1_b, router_w, exp_w1,
           exp_w3, exp_w2, ln2_g, ln2_b):
    bsz, seq, d = x.shape
    depth = w_in.shape[0]
    n_rwkv = shift_conv.shape[-1]
    r_w, r_a, r_g = w2.shape[2], a2.shape[2], g2.shape[1]
    n_exp = router_w.shape[-1]
    assert 2 * r_w == LANES and 2 * r_a == LANES and r_g == LANES and n_rwkv == 3 * d + 3 * LANES
    assert r_k.shape[2] == HEAD and d % LANES == 0
    alpha = (2 * depth) ** 0.25

    n_rows = -(-(bsz + 1) // 8) * 8
    cc = jnp.zeros((n_rows, d), _F32).at[:bsz].set(c).at[bsz].set(c_ctx)
    mods = _ada_modulation(cc, ada_w, ada_b, block_n=d)

    s_zero = jnp.zeros((bsz, d // LANES, LANES, LANES), _F32)
    vec = lambda p: p.reshape(1, d)

    for l in range(depth):
        mod_x = mods[l, :bsz].reshape(bsz, 1, 6, d)
        mod_c = mods[l, bsz].reshape(1, 1, 6, d)
        lat_mod = [mod_x[:, :, i] for i in range(6)]
        ctx_mod = [mod_c[:, :, i] for i in range(6)]
        w_rwkv = w_in[l, :, :n_rwkv].astype(_BF16)
        w_sgu = w_in[l, :, n_rwkv:n_rwkv + 2 * d].astype(_BF16)
        w_gate = w_in[l, :, n_rwkv + 2 * d:].astype(_BF16)
        prep_w = dict(conv=shift_conv[l], w0=w0[l],
                      w2f=_pad_rows(w2[l, 0], 0).astype(_BF16), w2b=_pad_rows(w2[l, 1], r_w).astype(_BF16),
                      a0=a0[l], a2f=_pad_rows(a2[l, 0], 0).astype(_BF16),
                      a2b=_pad_rows(a2[l, 1], r_a).astype(_BF16), g2=g2[l].astype(_BF16),
                      k_k=vec(k_k[l]), k_a=vec(k_a[l]))
        chunk = sgu_w.shape[-1]
        sgu_bias = jnp.repeat(sgu_b[l].T, d // sgu_w.shape[1], axis=1)
        mix_w = (vec(lnx_g[l]), vec(lnx_b[l]), r_k[l].reshape(1, d), w_branch_a[l].astype(_BF16),
                 w_branch_b[l].astype(_BF16), w_out[l].astype(_BF16), vec(ln1_g[l]), vec(ln1_b[l]))
        rw_pad = jnp.zeros((d, LANES), _F32).at[:, :n_exp].set(router_w[l])
        ew1, ew3, ew2 = exp_w1[l].astype(_BF16), exp_w3[l].astype(_BF16), exp_w2[l].astype(_BF16)

        def mixer(tok, mod, s0_f, s0_b, full):
            t = tok.shape[1]
            bt = _block_t(t, 512)
            z_r = _ln_mod_matmul(tok, mod[0], mod[1], w_rwkv, block_t=bt)
            r, v, kk, kf, kb, ldf, ldb, bef, beb, g = _rwkv_prep(z_r, d=d, block_t=_block_t(t, 256), **prep_w)
            y_f, s_f = _rwkv_scan(r, kf, v, ldf, kk, bef, s0_f, reverse=False, block_t=_block_t(t, 256))
            y_b, s_b = _rwkv_scan(r, kb, v, ldb, kk, beb, s0_b, reverse=True, block_t=_block_t(t, 256))
            if not full:
                return None, s_f, s_b
            z_s = _ln_mod_matmul(tok, mod[0], mod[1], w_sgu, block_t=bt)
            z_g = _ln_mod_matmul(tok, mod[0], mod[1], w_gate, block_t=bt)
            y_s = _spatial_gating(z_s, vec(sgu_ln_g[l]), vec(sgu_ln_b[l]), sgu_w[l].astype(_BF16), sgu_bias,
                                  block_t=_block_t(t, 2 * chunk))
            out = _merge(y_f, y_b, r, kf, v, g, y_s, z_g, tok, mod[2], *mix_w, alpha=alpha,
                         block_t=_block_t(t, 256))
            return out, s_f, s_b

        def moe(streams):
            sets = []
            for tok, mod in streams:
                h, aff = _router(tok, mod[3], mod[4], rw_pad, n_exp=n_exp, block_t=_block_t(tok.shape[1], 512))
                aff_t = jnp.swapaxes(aff[:, :, :n_exp], 1, 2)
                sets.append((h, aff_t[:, :, None, :], aff_t[:, :, :, None]))
            ys = _expert_choice_ffn(sets, ew1, ew3, ew2, block_f=_block_t(ew1.shape[-1], 1024))
            return [_post_norm(tok, y, mod[5], vec(ln2_g[l]), vec(ln2_b[l]), alpha=alpha,
                               block_t=_block_t(tok.shape[1], 512)) for (tok, mod), y in zip(streams, ys)]

        last = l == depth - 1
        ctx_mixed, s_f, s_b = mixer(ctx, ctx_mod, s_zero, s_zero, full=not last)
        x, _, _ = mixer(x, lat_mod, s_f, s_b, full=True)
        if last:
            x, = moe([(x, lat_mod)])
        else:
            x, ctx = moe([(x, lat_mod), (ctx_mixed, ctx_mod)])
    return x
```

```python
import functools
import math

import jax
import jax.numpy as jnp
from jax import lax
from jax.experimental import pallas as pl
from jax.experimental.pallas import tpu as pltpu

LANES = 128
SUBLANES = 8
HEAD = 64
SCAN_CHUNK = 64
INV_BLOCK = 16
CAP_FACTOR = 2
LN_EPS = 1e-5
GN_EPS = 64e-5
VMEM_LIMIT = 56 * 1024 * 1024

_F32 = jnp.float32
_BF16 = jnp.bfloat16


def _cparams(sem):
    return pltpu.CompilerParams(dimension_semantics=sem, vmem_limit_bytes=VMEM_LIMIT)


def _bdot(a, b):
    return jnp.dot(a.astype(_BF16), b.astype(_BF16), preferred_element_type=_F32)


def _stack_heads(x, lane_lo):
    return jnp.concatenate([jnp.where(lane_lo, x, 0.0), jnp.where(lane_lo, 0.0, x)], axis=0)


def _unit_tri_inverses(mats, eye):
    n = eye.shape[0]
    row = lax.broadcasted_iota(jnp.int32, (n, n), 0) // INV_BLOCK
    col = lax.broadcasted_iota(jnp.int32, (n, n), 1) // INV_BLOCK
    same = row == col
    p = [jnp.where(same, -a, 0.0) for a in mats]
    t = [eye + x for x in p]
    for _ in range(3):
        p = [_bdot(x, x) for x in p]
        t = [ti + _bdot(ti, x) for ti, x in zip(t, p)]
    e = [_bdot(ti, jnp.where(same, 0.0, a)) for ti, a in zip(t, mats)]
    e2 = [_bdot(x, x) for x in e]
    f = [eye - x + x2 - _bdot(x, x2) for x, x2 in zip(e, e2)]
    return [_bdot(fi, ti) for fi, ti in zip(f, t)]


def _scan_kernel(r_ref, k_ref, v_ref, ld_ref, kk_ref, be_ref, s0_ref, y_ref, s_ref,
                 q_scr, y0_scr, m_scr, n0_scr, dec_scr, *, reverse, n_chunks):
    t_idx = pl.program_id(1)

    @pl.when(t_idx == 0)
    def _():
        s_ref[...] = s0_ref[...]

    c = SCAN_CHUNK
    pairs = range(r_ref.shape[-1] // LANES)
    lanes = [slice(p * LANES, (p + 1) * LANES) for p in pairs]
    ri = lax.broadcasted_iota(jnp.int32, (c, c), 0)
    ci = lax.broadcasted_iota(jnp.int32, (c, c), 1)
    tri = (ci >= ri if reverse else ci <= ri).astype(_F32)
    lane_lo = lax.broadcasted_iota(jnp.int32, (c, LANES), 1) < HEAD
    r2 = lax.broadcasted_iota(jnp.int32, (2 * c, 2 * c), 0)
    c2 = lax.broadcasted_iota(jnp.int32, (2 * c, 2 * c), 1)
    eye = (r2 == c2).astype(_F32)
    strict = (c2 % c > r2 % c) if reverse else (c2 % c < r2 % c)
    read_mask = strict if reverse else (c2 % c <= r2 % c)
    stack = lambda x: _stack_heads(x, lane_lo)
    contract_rows = (((0,), (0,)), ((), ()))

    def chunk_terms(j, carry):
        rows = pl.ds(pl.multiple_of(j * c, c), c)
        c_in_all = jnp.dot(tri, ld_ref[0, rows, :], precision=lax.Precision.HIGHEST,
                           preferred_element_type=_F32)
        c_in = [c_in_all[:, s] for s in lanes]
        tot = [x[0:1, :] if reverse else x[c - 1:c, :] for x in c_in]
        e_ex = [jnp.exp(x - ld_ref[0, rows, s]) for x, s in zip(c_in, lanes)]
        e_rd = e_ex if reverse else [jnp.exp(x) for x in c_in]
        e_ninv = [jnp.exp(-x) for x in c_in]
        e_tot = [jnp.exp(t - x) for t, x in zip(tot, c_in)]
        kx = [k_ref[0, rows, s] for s in lanes]
        be = [be_ref[0, rows, s] for s in lanes]
        kap = [stack(kk_ref[0, rows, s] * e) for s, e in zip(lanes, e_ex)]
        rr = [stack(r_ref[0, rows, s] * e) for s, e in zip(lanes, e_rd)]
        lhs = [jnp.concatenate([a, b], axis=0).astype(_BF16) for a, b in zip(kap, rr)]
        rhs = [jnp.concatenate([stack(k * e), stack(b * e)], axis=0).astype(_BF16)
               for k, b, e in zip(kx, be, e_ninv)]
        aa = [lax.dot_general(a, b, (((1,), (1,)), ((), ())), preferred_element_type=_F32)
              for a, b in zip(lhs, rhs)]
        tf = _unit_tri_inverses([jnp.where(strict, x[:2 * c, 2 * c:], 0.0) for x in aa], eye)
        vm = [stack(v_ref[0, rows, s]).astype(_BF16) for s in lanes]
        a_v = [jnp.concatenate([jnp.where(strict, x[:2 * c, :2 * c], 0.0),
                                jnp.where(read_mask, x[2 * c:, :2 * c], 0.0)], axis=0) for x in aa]
        av = [_bdot(a, v) for a, v in zip(a_v, vm)]
        pu = [_bdot(t, jnp.concatenate([a, b[:2 * c]], axis=1)).astype(_BF16) for t, a, b in zip(tf, kap, av)]
        arb = [_bdot(jnp.where(read_mask, x[2 * c:, 2 * c:], 0.0), z) for x, z in zip(aa, pu)]
        bhat = [stack(b * e).astype(_BF16) for b, e in zip(be, e_tot)]
        khat = [stack(k * e).astype(_BF16) for k, e in zip(kx, e_tot)]
        bp = [lax.dot_general(b, z, contract_rows, preferred_element_type=_F32) for b, z in zip(bhat, pu)]
        kv = [lax.dot_general(k, v, contract_rows, preferred_element_type=_F32) for k, v in zip(khat, vm)]
        for p in pairs:
            q = rr[p] - arb[p][:, :2 * c]
            y0 = av[p][2 * c:] - arb[p][:, 2 * c:]
            q_scr[j, p] = (q[:c] + q[c:]).astype(_BF16)
            y0_scr[j, p] = y0[:c] + y0[c:]
            m_scr[j, p] = (-bp[p][:, :2 * c]).astype(_BF16)
            n0_scr[j, p] = kv[p] - bp[p][:, 2 * c:]
            dec = jnp.sum(jnp.where(eye > 0, jnp.broadcast_to(jnp.exp(tot[p]), (2 * c, LANES)), 0.0),
                          axis=1, keepdims=True)
            dec_scr[j, p] = jnp.broadcast_to(dec, (2 * c, LANES))
        return carry

    lax.fori_loop(0, n_chunks, chunk_terms, 0)

    def advance(ic, carry):
        j = (n_chunks - 1 - ic) if reverse else ic
        rows = pl.ds(pl.multiple_of(j * c, c), c)
        s = [s_ref[0, p] for p in pairs]
        sb = [x.astype(_BF16) for x in s]
        y = [jnp.dot(q_scr[j, p], sb[p], preferred_element_type=_F32) + y0_scr[j, p] for p in pairs]
        s_new = [dec_scr[j, p] * s[p] + jnp.dot(m_scr[j, p], sb[p], preferred_element_type=_F32) + n0_scr[j, p]
                 for p in pairs]
        for p in pairs:
            y_ref[0, rows, lanes[p]] = y[p]
            s_ref[0, p] = s_new[p]
        return carry

    lax.fori_loop(0, n_chunks, advance, 0)


def _rwkv_scan(r, k, v, logd, kk, beta, s0, *, reverse, block_t):
    b, t, d = r.shape
    assert t % block_t == 0 and block_t % SCAN_CHUNK == 0 and d % LANES == 0 and 2 * HEAD == LANES
    nt = t // block_t
    tmap = (lambda bi, ti: (bi, nt - 1 - ti, 0)) if reverse else (lambda bi, ti: (bi, ti, 0))
    tok = pl.BlockSpec((1, block_t, d), tmap)
    st = pl.BlockSpec((1, d // LANES, LANES, LANES), lambda bi, ti: (bi, 0, 0, 0))
    n_chunks, n_pairs = block_t // SCAN_CHUNK, d // LANES
    return pl.pallas_call(
        functools.partial(_scan_kernel, reverse=reverse, n_chunks=n_chunks),
        grid=(b, nt),
        in_specs=[tok] * 6 + [st],
        out_specs=[tok, st],
        out_shape=[jax.ShapeDtypeStruct((b, t, d), _F32),
                   jax.ShapeDtypeStruct((b, d // LANES, LANES, LANES), _F32)],
        scratch_shapes=[pltpu.VMEM((n_chunks, n_pairs, SCAN_CHUNK, LANES), _BF16),
                        pltpu.VMEM((n_chunks, n_pairs, SCAN_CHUNK, LANES), _F32),
                        pltpu.VMEM((n_chunks, n_pairs, LANES, LANES), _BF16),
                        pltpu.VMEM((n_chunks, n_pairs, LANES, LANES), _F32),
                        pltpu.VMEM((n_chunks, n_pairs, LANES, LANES), _F32)],
        compiler_params=_cparams(("parallel", "arbitrary")),
        name="rwkv_scan_rev" if reverse else "rwkv_scan_fwd",
    )(r, k, v, logd, kk, beta, s0)


def _layer_norm(x, eps=LN_EPS):
    mu = jnp.mean(x, axis=-1, keepdims=True)
    xc = x - mu
    var = jnp.mean(xc * xc, axis=-1, keepdims=True)
    return xc * lax.rsqrt(var + eps)


def _sigmoid(x):
    return 0.5 * (1.0 + jnp.tanh(0.5 * x))


def _head_sum(x):
    row = lax.broadcasted_iota(jnp.int32, (LANES, LANES), 0) // HEAD
    col = lax.broadcasted_iota(jnp.int32, (LANES, LANES), 1) // HEAD
    ones = (row == col).astype(_BF16)
    hi = x.astype(_BF16)
    lo = (x - hi.astype(_F32)).astype(_BF16)
    slabs = []
    for s in range(0, x.shape[-1], LANES):
        slabs.append(jnp.dot(hi[:, s:s + LANES], ones, preferred_element_type=_F32)
                     + jnp.dot(lo[:, s:s + LANES], ones, preferred_element_type=_F32))
    return jnp.concatenate(slabs, axis=-1)


def _mod_map(n_mod):
    return (lambda bi, ti: (bi, 0, 0)) if n_mod > 1 else (lambda bi, ti: (0, 0, 0))


def _vec_spec(d):
    return pl.BlockSpec((1, d), lambda bi, ti: (0, 0))


def _full_spec(shape):
    return pl.BlockSpec(shape, lambda bi, ti: (0,) * len(shape))


def _tok_spec(tm, d):
    return pl.BlockSpec((1, tm, d), lambda bi, ti: (bi, ti, 0))


def _ada_kernel(c_ref, w_ref, b_ref, o_ref):
    c = c_ref[...]
    o_ref[0] = _bdot(c * _sigmoid(c), w_ref[0]) + b_ref[0]


def _ada_modulation(cc, ada_w, ada_b, *, block_n):
    depth, d, n = ada_w.shape
    rows = cc.shape[0]
    return pl.pallas_call(
        _ada_kernel,
        grid=(depth, n // block_n),
        in_specs=[pl.BlockSpec((rows, d), lambda li, ni: (0, 0)),
                  pl.BlockSpec((1, d, block_n), lambda li, ni: (li, 0, ni)),
                  pl.BlockSpec((1, 1, block_n), lambda li, ni: (li, 0, ni))],
        out_specs=pl.BlockSpec((1, rows, block_n), lambda li, ni: (li, 0, ni)),
        out_shape=jax.ShapeDtypeStruct((depth, rows, n), _F32),
        compiler_params=_cparams(("parallel", "parallel")),
        name="ada_modulation",
    )(cc, ada_w, ada_b.reshape(depth, 1, n))


def _ln_mod_matmul_kernel(x_ref, sh_ref, sc_ref, w_ref, o_ref):
    h = _layer_norm(x_ref[0]) * (1.0 + sc_ref[0]) + sh_ref[0]
    o_ref[0] = _bdot(h, w_ref[...])


def _ln_mod_matmul(x, shift, scale, w, *, block_t):
    b, t, d = x.shape
    n = w.shape[1]
    mod = pl.BlockSpec((1, 1, d), _mod_map(shift.shape[0]))
    return pl.pallas_call(
        _ln_mod_matmul_kernel,
        grid=(b, t // block_t),
        in_specs=[_tok_spec(block_t, d), mod, mod, _full_spec((d, n))],
        out_specs=_tok_spec(block_t, n),
        out_shape=jax.ShapeDtypeStruct((b, t, n), _F32),
        compiler_params=_cparams(("parallel", "parallel")),
        name="ln_mod_matmul",
    )(x, shift, scale, w)


def _prep_kernel(z_ref, zp_ref, zn_ref, conv_ref, w0_ref, w2f_ref, w2b_ref, a0_ref, a2f_ref, a2b_ref,
                 g2_ref, kk_ref, ka_ref,
                 r_o, v_o, kkn_o, kf_o, kb_o, ldf_o, ldb_o, bf_o, bb_o, g_o):
    z = z_ref[0]
    tm = z.shape[0]
    d = r_o.shape[-1]
    row = lax.broadcasted_iota(jnp.int32, z.shape, 0)
    t_idx = pl.program_id(1)
    before = jnp.where(t_idx == 0, 0.0, zp_ref[0, 7:8, :])
    after = jnp.where(t_idx == pl.num_programs(1) - 1, 0.0, zn_ref[0, 0:1, :])
    up = jnp.where(row == 0, before, pltpu.roll(z, 1, 0))
    dn = jnp.where(row == tm - 1, after, pltpu.roll(z, tm - 1, 0))
    conv = conv_ref[...]
    zs = up * conv[0:1] + z * conv[1:2] + dn * conv[2:3]
    r = zs[:, 0:d]
    k = zs[:, d:2 * d]
    v = zs[:, 2 * d:3 * d]
    tw = jnp.tanh(zs[:, 3 * d:3 * d + LANES])
    la = zs[:, 3 * d + LANES:3 * d + 2 * LANES]
    dg = zs[:, 3 * d + 2 * LANES:3 * d + 3 * LANES]

    def log_decay(w0, w2_ref):
        return -math.exp(-0.5) * _sigmoid(w0 + _bdot(tw, w2_ref[...]))

    w0 = w0_ref[...]
    a0 = a0_ref[...]
    a_f = _sigmoid(a0[0:1] + _bdot(la, a2f_ref[...]))
    a_b = _sigmoid(a0[1:2] + _bdot(la, a2b_ref[...]))
    kkr = k * kk_ref[...]
    kkn = kkr * lax.rsqrt(jnp.maximum(_head_sum(kkr * kkr), 1e-12))
    ka = ka_ref[...]
    r_o[0] = r
    v_o[0] = v
    kkn_o[0] = kkn
    kf_o[0] = k * (1.0 + (a_f - 1.0) * ka)
    kb_o[0] = k * (1.0 + (a_b - 1.0) * ka)
    ldf_o[0] = log_decay(w0[0:1], w2f_ref)
    ldb_o[0] = log_decay(w0[1:2], w2b_ref)
    bf_o[0] = kkn * a_f
    bb_o[0] = kkn * a_b
    g_o[0] = _bdot(_sigmoid(dg), g2_ref[...])


def _rwkv_prep(z, conv, w0, w2f, w2b, a0, a2f, a2b, g2, k_k, k_a, *, d, block_t):
    b, t, nr = z.shape
    nb = t // block_t
    tiles = block_t // SUBLANES
    prev_tile = pl.BlockSpec((1, SUBLANES, nr), lambda bi, ti: (bi, jnp.maximum(ti * tiles - 1, 0), 0))
    next_tile = pl.BlockSpec((1, SUBLANES, nr), lambda bi, ti: (bi, jnp.minimum((ti + 1) * tiles, nb * tiles - 1), 0))
    out = jax.ShapeDtypeStruct((b, t, d), _F32)
    return pl.pallas_call(
        _prep_kernel,
        grid=(b, nb),
        in_specs=[_tok_spec(block_t, nr), prev_tile, next_tile, _full_spec(conv.shape), _full_spec(w0.shape),
                  _full_spec(w2f.shape), _full_spec(w2b.shape), _full_spec(a0.shape), _full_spec(a2f.shape),
                  _full_spec(a2b.shape), _full_spec(g2.shape), _vec_spec(d), _vec_spec(d)],
        out_specs=[_tok_spec(block_t, d)] * 10,
        out_shape=[out] * 10,
        compiler_params=_cparams(("parallel", "parallel")),
        name="rwkv_prep",
    )(z, z, z, conv, w0, w2f, w2b, a0, a2f, a2b, g2, k_k, k_a)


def _sgu_kernel(z_ref, lg_ref, lb_ref, w_ref, bias_ref, o_ref, *, chunk):
    z = z_ref[0]
    ge = 0.5 * z * (1.0 + lax.erf(z * (2.0 ** -0.5)))
    d = o_ref.shape[-1]
    u = ge[:, :d]
    v = _layer_norm(ge[:, d:]) * lg_ref[...] + lb_ref[...]
    bias = bias_ref[...]
    for c in range(z.shape[0] // chunk):
        rows = slice(c * chunk, (c + 1) * chunk)
        for g in range(w_ref.shape[0]):
            cols = slice(g * (d // w_ref.shape[0]), (g + 1) * (d // w_ref.shape[0]))
            mixed = _bdot(w_ref[g], v[rows, cols]) + bias[:, cols]
            o_ref[0, rows, cols] = u[rows, cols] * mixed


def _spatial_gating(z_sgu, ln_g, ln_b, w, bias_full, *, block_t):
    b, t, d2 = z_sgu.shape
    d = d2 // 2
    chunk = w.shape[-1]
    return pl.pallas_call(
        functools.partial(_sgu_kernel, chunk=chunk),
        grid=(b, t // block_t),
        in_specs=[_tok_spec(block_t, d2), _vec_spec(d), _vec_spec(d), _full_spec(w.shape),
                  _full_spec(bias_full.shape)],
        out_specs=_tok_spec(block_t, d),
        out_shape=jax.ShapeDtypeStruct((b, t, d), _F32),
        compiler_params=_cparams(("parallel", "parallel")),
        name="spatial_gating",
    )(z_sgu, ln_g, ln_b, w, bias_full)


def _merge_kernel(yf_ref, yb_ref, r_ref, kf_ref, v_ref, g_ref, ys_ref, zg_ref, x_ref, gt_ref,
                  lnxg_ref, lnxb_ref, rk_ref, wa_ref, wb_ref, wo_ref, l1g_ref, l1b_ref, o_ref, *, alpha):
    d = o_ref.shape[-1]
    y = yf_ref[0] + yb_ref[0]
    yc = y - _head_sum(y) * (1.0 / HEAD)
    var = _head_sum(yc * yc) * (1.0 / HEAD)
    yn = yc * lax.rsqrt(var + GN_EPS) * lnxg_ref[...] + lnxb_ref[...]
    bonus = _head_sum(r_ref[0] * kf_ref[0] * rk_ref[...]) * v_ref[0]
    y_a = (yn + bonus) * g_ref[0]
    zg = zg_ref[0]
    merged = _sigmoid(zg[:, :d]) * _bdot(y_a, wa_ref[...]) + _sigmoid(zg[:, d:]) * _bdot(ys_ref[0], wb_ref[...])
    out = _bdot(merged, wo_ref[...])
    o_ref[0] = _layer_norm(alpha * x_ref[0] + gt_ref[0] * out) * l1g_ref[...] + l1b_ref[...]


def _merge(yf, yb, r, kf, v, g, ys, zg, x, gate, lnx_g, lnx_b, r_k, wa, wb, wo, l1g, l1b, *, alpha, block_t):
    b, t, d = x.shape
    tok = _tok_spec(block_t, d)
    vec = _vec_spec(d)
    mat = _full_spec((d, d))
    return pl.pallas_call(
        functools.partial(_merge_kernel, alpha=alpha),
        grid=(b, t // block_t),
        in_specs=[tok] * 7 + [_tok_spec(block_t, 2 * d), tok, pl.BlockSpec((1, 1, d), _mod_map(gate.shape[0])),
                  vec, vec, vec, mat, mat, mat, vec, vec],
        out_specs=tok,
        out_shape=jax.ShapeDtypeStruct((b, t, d), _F32),
        compiler_params=_cparams(("parallel", "parallel")),
        name="branch_merge",
    )(yf, yb, r, kf, v, g, ys, zg, x, gate, lnx_g, lnx_b, r_k, wa, wb, wo, l1g, l1b)


def _router_kernel(x_ref, sh_ref, sc_ref, rw_ref, h_ref, aff_ref, *, n_exp):
    h = _layer_norm(x_ref[0]) * (1.0 + sc_ref[0]) + sh_ref[0]
    h_ref[0] = h.astype(_BF16)
    logits = jnp.dot(h, rw_ref[...], precision=lax.Precision.HIGHEST, preferred_element_type=_F32)
    lane = lax.broadcasted_iota(jnp.int32, logits.shape, 1)
    logits = jnp.where(lane < n_exp, logits, -jnp.inf)
    e = jnp.exp(logits - jnp.max(logits, axis=-1, keepdims=True))
    aff_ref[0] = e / jnp.sum(e, axis=-1, keepdims=True)


def _router(x, shift, scale, rw_pad, *, n_exp, block_t):
    b, t, d = x.shape
    mod = pl.BlockSpec((1, 1, d), _mod_map(shift.shape[0]))
    return pl.pallas_call(
        functools.partial(_router_kernel, n_exp=n_exp),
        grid=(b, t // block_t),
        in_specs=[_tok_spec(block_t, d), mod, mod, _full_spec(rw_pad.shape)],
        out_specs=[_tok_spec(block_t, d), _tok_spec(block_t, LANES)],
        out_shape=[jax.ShapeDtypeStruct((b, t, d), _BF16), jax.ShapeDtypeStruct((b, t, LANES), _F32)],
        compiler_params=_cparams(("parallel", "parallel")),
        name="moe_router",
    )(x, shift, scale, rw_pad)


def _select_kernel(aff_ref, before_ref, slot_ref, *, cap):
    a = aff_ref[0]
    key = lax.bitcast_convert_type(a, jnp.int32)
    thr = jnp.zeros((a.shape[0], 1), jnp.int32)
    for bit in range(30, -1, -1):
        cand = thr | (1 << bit)
        count = jnp.sum((key >= cand).astype(_F32), axis=1, keepdims=True)
        thr = jnp.where(count >= cap, cand, thr)
    above = key > thr
    tied = key == thr
    need = cap - jnp.sum(above.astype(_F32), axis=1, keepdims=True)
    before = before_ref[...]
    tied_before = jnp.dot(tied.astype(_BF16), before, preferred_element_type=_F32)
    chosen = jnp.logical_or(above, jnp.logical_and(tied, tied_before < need))
    slot = jnp.dot(chosen.astype(_BF16), before, preferred_element_type=_F32)
    slot_ref[0] = jnp.where(chosen, slot, -1.0)


def _expert_choice_slots(aff_t, *, cap):
    b, n_exp, t = aff_t.shape
    before = jnp.triu(jnp.ones((t, t), _BF16), k=1)
    return pl.pallas_call(
        functools.partial(_select_kernel, cap=cap),
        grid=(b,),
        in_specs=[pl.BlockSpec((1, n_exp, t), lambda bi: (bi, 0, 0)), pl.BlockSpec((t, t), lambda bi: (0, 0))],
        out_specs=pl.BlockSpec((1, n_exp, t), lambda bi: (bi, 0, 0)),
        out_shape=jax.ShapeDtypeStruct((b, n_exp, t), _F32),
        compiler_params=_cparams(("parallel",)),
        name="expert_choice_slots",
    )(aff_t, before)


def _moe_kernel(*refs, n_sets):
    ins, rest = refs[:3 * n_sets], refs[3 * n_sets:]
    w1_ref, w3_ref, w2_ref = rest[:3]
    outs = rest[3:3 + n_sets]
    xe_ref, ye_ref = rest[3 + n_sets:5 + n_sets]
    p_refs = rest[5 + n_sets:]
    e_idx = pl.program_id(1)
    f_idx = pl.program_id(2)
    caps = [p.shape[0] for p in p_refs]
    offs = [sum(caps[:i]) for i in range(n_sets)]

    @pl.when(jnp.logical_and(e_idx == 0, f_idx == 0))
    def _():
        for o_ref in outs:
            o_ref[...] = jnp.zeros_like(o_ref)

    @pl.when(f_idx == 0)
    def _():
        for i in range(n_sets):
            h_ref, slot_ref = ins[3 * i:3 * i + 2]
            cap, t = p_refs[i].shape
            slot = lax.broadcasted_iota(jnp.int32, (cap, t), 0).astype(_F32)
            p = (slot_ref[0, 0] == slot).astype(_BF16)
            p_refs[i][...] = p
            xe_ref[offs[i]:offs[i] + cap, :] = jnp.dot(p, h_ref[0], preferred_element_type=_F32).astype(_BF16)
        ye_ref[...] = jnp.zeros_like(ye_ref)

    xe = xe_ref[...]
    h1 = jnp.dot(xe, w1_ref[0], preferred_element_type=_F32)
    h3 = jnp.dot(xe, w3_ref[0], preferred_element_type=_F32)
    ye_ref[...] += _bdot(h1 * _sigmoid(h1) * h3, w2_ref[0])

    @pl.when(f_idx == pl.num_programs(2) - 1)
    def _():
        for i in range(n_sets):
            acol_ref, o_ref = ins[3 * i + 2], outs[i]
            cap, t = p_refs[i].shape
            blk = min(t, 512)
            ye = ye_ref[offs[i]:offs[i] + cap, :].astype(_BF16)
            for t0 in range(0, t, blk):
                back = lax.dot_general(p_refs[i][:, t0:t0 + blk], ye, (((0,), (0,)), ((), ())),
                                       preferred_element_type=_F32)
                o_ref[0, t0:t0 + blk, :] += acol_ref[0, 0, t0:t0 + blk, :] * back


def _expert_choice_ffn(sets, w1, w3, w2, *, block_f):
    n_exp, d, d_ff = w1.shape
    b = sets[0][0].shape[0]
    caps = [CAP_FACTOR * h.shape[1] // n_exp for h, _, _ in sets]
    in_specs, out_specs, out_shape, operands = [], [], [], []
    for h, a_row, a_col in sets:
        t = h.shape[1]
        in_specs += [pl.BlockSpec((1, t, d), lambda bi, ei, fi: (bi, 0, 0)),
                     pl.BlockSpec((1, 1, 1, t), lambda bi, ei, fi: (bi, ei, 0, 0)),
                     pl.BlockSpec((1, 1, t, 1), lambda bi, ei, fi: (bi, ei, 0, 0))]
        out_specs.append(pl.BlockSpec((1, t, d), lambda bi, ei, fi: (bi, 0, 0)))
        out_shape.append(jax.ShapeDtypeStruct((b, t, d), _F32))
        operands += [h, a_row, a_col]
    in_specs += [pl.BlockSpec((1, d, block_f), lambda bi, ei, fi: (ei, 0, fi)),
                 pl.BlockSpec((1, d, block_f), lambda bi, ei, fi: (ei, 0, fi)),
                 pl.BlockSpec((1, block_f, d), lambda bi, ei, fi: (ei, fi, 0))]
    return pl.pallas_call(
        functools.partial(_moe_kernel, n_sets=len(sets)),
        grid=(b, n_exp, d_ff // block_f),
        in_specs=in_specs,
        out_specs=out_specs,
        out_shape=out_shape,
        scratch_shapes=[pltpu.VMEM((sum(caps), d), _BF16), pltpu.VMEM((sum(caps), d), _F32)]
        + [pltpu.VMEM((cap, h.shape[1]), _BF16) for cap, (h, _, _) in zip(caps, sets)],
        compiler_params=_cparams(("parallel", "arbitrary", "arbitrary")),
        name="expert_choice_ffn",
    )(*operands, w1, w3, w2)


def _post_norm_kernel(x_ref, y_ref, gt_ref, g_ref, b_ref, o_ref, *, alpha):
    o_ref[0] = _layer_norm(alpha * x_ref[0] + gt_ref[0] * y_ref[0]) * g_ref[...] + b_ref[...]


def _post_norm(x, y, gate, g, bias, *, alpha, block_t):
    b, t, d = x.shape
    tok = _tok_spec(block_t, d)
    return pl.pallas_call(
        functools.partial(_post_norm_kernel, alpha=alpha),
        grid=(b, t // block_t),
        in_specs=[tok, tok, pl.BlockSpec((1, 1, d), _mod_map(gate.shape[0])), _vec_spec(d), _vec_spec(d)],
        out_specs=tok,
        out_shape=jax.ShapeDtypeStruct((b, t, d), _F32),
        compiler_params=_cparams(("parallel", "parallel")),
        name="post_norm",
    )(x, y, gate, g, bias)


def _block_t(t, want):
    return want if t % want == 0 else t


def _pad_rows(w, top):
    out = jnp.zeros((LANES, w.shape[1]), w.dtype)
    return lax.dynamic_update_slice(out, w, (top, 0))


def kernel(x, c, ctx, c_ctx, ada_w, ada_b, w_in, shift_conv, w0, w2, a0, a2, g2, k_k, k_a, r_k, lnx_g, lnx_b,
           sgu_ln_g, sgu_ln_b, sgu_w, sgu_b, w_branch_a, w_branch_b, w_out, ln1_g, ln1_b, router_w, exp_w1,
           exp_w3, exp_w2, ln2_g, ln2_b):
    bsz, seq, d = x.shape
    depth = w_in.shape[0]
    n_rwkv = shift_conv.shape[-1]
    r_w, r_a, r_g = w2.shape[2], a2.shape[2], g2.shape[1]
    n_exp = router_w.shape[-1]
    assert 2 * r_w == LANES and 2 * r_a == LANES and r_g == LANES and n_rwkv == 3 * d + 3 * LANES
    assert r_k.shape[2] == HEAD and d % LANES == 0
    alpha = (2 * depth) ** 0.25

    n_rows = -(-(bsz + 1) // 8) * 8
    cc = jnp.zeros((n_rows, d), _F32).at[:bsz].set(c).at[bsz].set(c_ctx)
    mods = _ada_modulation(cc, ada_w, ada_b, block_n=d)

    s_zero = jnp.zeros((bsz, d // LANES, LANES, LANES), _F32)
    vec = lambda p: p.reshape(1, d)

    for l in range(depth):
        mod_x = mods[l, :bsz].reshape(bsz, 1, 6, d)
        mod_c = mods[l, bsz].reshape(1, 1, 6, d)
        lat_mod = [mod_x[:, :, i] for i in range(6)]
        ctx_mod = [mod_c[:, :, i] for i in range(6)]
        w_rwkv = w_in[l, :, :n_rwkv].astype(_BF16)
        w_sgu = w_in[l, :, n_rwkv:n_rwkv + 2 * d].astype(_BF16)
        w_gate = w_in[l, :, n_rwkv + 2 * d:].astype(_BF16)
        prep_w = dict(conv=shift_conv[l], w0=w0[l],
                      w2f=_pad_rows(w2[l, 0], 0).astype(_BF16), w2b=_pad_rows(w2[l, 1], r_w).astype(_BF16),
                      a0=a0[l], a2f=_pad_rows(a2[l, 0], 0).astype(_BF16),
                      a2b=_pad_rows(a2[l, 1], r_a).astype(_BF16), g2=g2[l].astype(_BF16),
                      k_k=vec(k_k[l]), k_a=vec(k_a[l]))
        chunk = sgu_w.shape[-1]
        sgu_bias = jnp.repeat(sgu_b[l].T, d // sgu_w.shape[1], axis=1)
        mix_w = (vec(lnx_g[l]), vec(lnx_b[l]), r_k[l].reshape(1, d), w_branch_a[l].astype(_BF16),
                 w_branch_b[l].astype(_BF16), w_out[l].astype(_BF16), vec(ln1_g[l]), vec(ln1_b[l]))
        rw_pad = jnp.zeros((d, LANES), _F32).at[:, :n_exp].set(router_w[l])
        ew1, ew3, ew2 = exp_w1[l].astype(_BF16), exp_w3[l].astype(_BF16), exp_w2[l].astype(_BF16)

        def mixer(tok, mod, s0_f, s0_b, full):
            t = tok.shape[1]
            bt = _block_t(t, 512)
            z_r = _ln_mod_matmul(tok, mod[0], mod[1], w_rwkv, block_t=bt)
            r, v, kk, kf, kb, ldf, ldb, bef, beb, g = _rwkv_prep(z_r, d=d, block_t=_block_t(t, 256), **prep_w)
            y_f, s_f = _rwkv_scan(r, kf, v, ldf, kk, bef, s0_f, reverse=False, block_t=_block_t(t, 256))
            y_b, s_b = _rwkv_scan(r, kb, v, ldb, kk, beb, s0_b, reverse=True, block_t=_block_t(t, 256))
            if not full:
                return None, s_f, s_b
            z_s = _ln_mod_matmul(tok, mod[0], mod[1], w_sgu, block_t=bt)
            z_g = _ln_mod_matmul(tok, mod[0], mod[1], w_gate, block_t=bt)
            y_s = _spatial_gating(z_s, vec(sgu_ln_g[l]), vec(sgu_ln_b[l]), sgu_w[l].astype(_BF16), sgu_bias,
                                  block_t=_block_t(t, 2 * chunk))
            out = _merge(y_f, y_b, r, kf, v, g, y_s, z_g, tok, mod[2], *mix_w, alpha=alpha,
                         block_t=_block_t(t, 256))
            return out, s_f, s_b

        def moe(streams):
            sets = []
            for tok, mod in streams:
                h, aff = _router(tok, mod[3], mod[4], rw_pad, n_exp=n_exp, block_t=_block_t(tok.shape[1], 512))
                aff_t = jnp.swapaxes(aff[:, :, :n_exp], 1, 2)
                slots = _expert_choice_slots(aff_t, cap=CAP_FACTOR * tok.shape[1] // n_exp)
                sets.append((h, slots[:, :, None, :], aff_t[:, :, :, None]))
            ys = _expert_choice_ffn(sets, ew1, ew3, ew2, block_f=_block_t(ew1.shape[-1], 1024))
            return [_post_norm(tok, y, mod[5], vec(ln2_g[l]), vec(ln2_b[l]), alpha=alpha,
                               block_t=_block_t(tok.shape[1], 512)) for (tok, mod), y in zip(streams, ys)]

        last = l == depth - 1
        ctx_mixed, s_f, s_b = mixer(ctx, ctx_mod, s_zero, s_zero, full=not last)
        x, _, _ = mixer(x, lat_mod, s_f, s_b, full=True)
        if last:
            x, = moe([(x, lat_mod)])
        else:
            x, ctx = moe([(x, lat_mod), (ctx_mixed, ctx_mod)])
    return x
```

```python
import functools
import math

import jax
import jax.numpy as jnp
from jax import lax
from jax.experimental import pallas as pl
from jax.experimental.pallas import tpu as pltpu

LANES = 128
SUBLANES = 8
HEAD = 64
SCAN_CHUNK = 64
INV_BLOCK = 16
CAP_FACTOR = 2
LN_EPS = 1e-5
GN_EPS = 64e-5
VMEM_LIMIT = 56 * 1024 * 1024

_F32 = jnp.float32
_BF16 = jnp.bfloat16


def _cparams(sem):
    return pltpu.CompilerParams(dimension_semantics=sem, vmem_limit_bytes=VMEM_LIMIT)


def _bdot(a, b):
    return jnp.dot(a.astype(_BF16), b.astype(_BF16), preferred_element_type=_F32)


def _stack_heads(x, lane_lo):
    return jnp.concatenate([jnp.where(lane_lo, x, 0.0), jnp.where(lane_lo, 0.0, x)], axis=0)


def _pdot(a, b):
    bb = b.astype(_BF16)
    b0, b1 = bb[:, :LANES], bb[:, LANES:]
    z = jnp.zeros_like(b0)
    rhs = jnp.concatenate([jnp.concatenate([b0, z], axis=1), jnp.concatenate([z, b1], axis=1)], axis=0)
    return jnp.dot(a.astype(_BF16), rhs, preferred_element_type=_F32)


def _unit_tri_inverses(mats):
    n = mats[0].shape[0]
    row = lax.broadcasted_iota(jnp.int32, (n, 2 * n), 0)
    col = lax.broadcasted_iota(jnp.int32, (n, 2 * n), 1) % n
    eye = (row == col).astype(_F32)
    same = row // INV_BLOCK == col // INV_BLOCK
    p = [jnp.where(same, -a, 0.0) for a in mats]
    t = [eye + x for x in p]
    for _ in range(3):
        p = [_pdot(x, x) for x in p]
        t = [ti + _pdot(ti, x) for ti, x in zip(t, p)]
    e = [_pdot(ti, jnp.where(same, 0.0, a)) for ti, a in zip(t, mats)]
    e2 = [_pdot(x, x) for x in e]
    f = [eye - x + x2 - _pdot(x, x2) for x, x2 in zip(e, e2)]
    return [_pdot(fi, ti) for fi, ti in zip(f, t)]


def _scan_kernel(r_ref, k_ref, v_ref, ld_ref, kk_ref, be_ref, s0_ref, y_ref, s_ref,
                 q_scr, y0_scr, m_scr, n0_scr, dec_scr, *, reverse, n_chunks):
    t_idx = pl.program_id(1)

    @pl.when(t_idx == 0)
    def _():
        s_ref[...] = s0_ref[...]

    c = SCAN_CHUNK
    pairs = range(r_ref.shape[-1] // LANES)
    lanes = [slice(p * LANES, (p + 1) * LANES) for p in pairs]
    ri = lax.broadcasted_iota(jnp.int32, (c, c), 0)
    ci = lax.broadcasted_iota(jnp.int32, (c, c), 1)
    tri = (ci >= ri if reverse else ci <= ri).astype(_F32)
    lane_lo = lax.broadcasted_iota(jnp.int32, (c, LANES), 1) < HEAD
    r2 = lax.broadcasted_iota(jnp.int32, (2 * c, 2 * c), 0)
    c2 = lax.broadcasted_iota(jnp.int32, (2 * c, 2 * c), 1)
    eye = (r2 == c2).astype(_F32)
    strict = (c2 % c > r2 % c) if reverse else (c2 % c < r2 % c)
    read_mask = strict if reverse else (c2 % c <= r2 % c)
    stack = lambda x: _stack_heads(x, lane_lo)
    contract_rows = (((0,), (0,)), ((), ()))

    group = 2 if n_chunks % 2 == 0 else 1
    pack = lambda xs: [jnp.concatenate([xs[i], xs[i + 1]], axis=1) for i in range(0, len(xs), 2)]
    unpack = lambda xs: [h for x in xs for h in (x[:, :LANES], x[:, LANES:])]

    def chunk_terms(g, carry):
        js = [g * group + dj for dj in range(group)]
        rows_of = [pl.ds(pl.multiple_of(j * c, c), c) for j in js]
        c_all = [jnp.dot(tri, ld_ref[0, rows, :], precision=lax.Precision.HIGHEST, preferred_element_type=_F32)
                 for rows in rows_of]
        units = [(dj, p) for dj in range(group) for p in pairs]
        load = lambda ref: [ref[0, rows_of[dj], lanes[p]].astype(_F32) for dj, p in units]
        c_in = [c_all[dj][:, lanes[p]] for dj, p in units]
        tot = [x[0:1, :] if reverse else x[c - 1:c, :] for x in c_in]
        e_ex = [jnp.exp(x - l) for x, l in zip(c_in, load(ld_ref))]
        e_rd = e_ex if reverse else [jnp.exp(x) for x in c_in]
        e_ninv = [jnp.exp(-x) for x in c_in]
        e_tot = [jnp.exp(t - x) for t, x in zip(tot, c_in)]
        kx = load(k_ref)
        be = load(be_ref)
        kap = [stack(x * e) for x, e in zip(load(kk_ref), e_ex)]
        rr = [stack(x * e) for x, e in zip(load(r_ref), e_rd)]
        lhs = [jnp.concatenate([a, b], axis=0).astype(_BF16) for a, b in zip(kap, rr)]
        rhs = [jnp.concatenate([stack(k * e), stack(b * e)], axis=0).astype(_BF16)
               for k, b, e in zip(kx, be, e_ninv)]
        aa = [lax.dot_general(a, b, (((1,), (1,)), ((), ())), preferred_element_type=_F32)
              for a, b in zip(lhs, rhs)]
        tf = unpack(_unit_tri_inverses(pack([jnp.where(strict, x[:2 * c, 2 * c:], 0.0) for x in aa])))
        vm = [stack(x) for x in load(v_ref)]
        a_v = [jnp.concatenate([jnp.where(strict, x[:2 * c, :2 * c], 0.0),
                                jnp.where(read_mask, x[2 * c:, :2 * c], 0.0)], axis=0) for x in aa]
        av = unpack([_pdot(a, v) for a, v in zip(pack(a_v), pack(vm))])
        pu = [_bdot(t, jnp.concatenate([a, b[:2 * c]], axis=1)).astype(_BF16) for t, a, b in zip(tf, kap, av)]
        arb = [_bdot(jnp.where(read_mask, x[2 * c:, 2 * c:], 0.0), z) for x, z in zip(aa, pu)]
        bhat = [stack(b * e).astype(_BF16) for b, e in zip(be, e_tot)]
        khat = [stack(k * e).astype(_BF16) for k, e in zip(kx, e_tot)]
        bp = [lax.dot_general(b, z, contract_rows, preferred_element_type=_F32) for b, z in zip(bhat, pu)]
        kv = [lax.dot_general(k, v.astype(_BF16), contract_rows, preferred_element_type=_F32)
              for k, v in zip(khat, vm)]
        for u, (dj, p) in enumerate(units):
            j = js[dj]
            q = rr[u] - arb[u][:, :2 * c]
            y0 = av[u][2 * c:] - arb[u][:, 2 * c:]
            q_scr[j, p] = (q[:c] + q[c:]).astype(_BF16)
            y0_scr[j, p] = y0[:c] + y0[c:]
            m_scr[j, p] = (-bp[u][:, :2 * c]).astype(_BF16)
            n0_scr[j, p] = kv[u] - bp[u][:, 2 * c:]
            dec = jnp.sum(jnp.where(eye > 0, jnp.broadcast_to(jnp.exp(tot[u]), (2 * c, LANES)), 0.0),
                          axis=1, keepdims=True)
            dec_scr[j, p] = jnp.broadcast_to(dec, (2 * c, LANES))
        return carry

    lax.fori_loop(0, n_chunks // group, chunk_terms, 0)

    def advance(ic, carry):
        j = (n_chunks - 1 - ic) if reverse else ic
        rows = pl.ds(pl.multiple_of(j * c, c), c)
        s = [s_ref[0, p] for p in pairs]
        sb = [x.astype(_BF16) for x in s]
        y = [jnp.dot(q_scr[j, p], sb[p], preferred_element_type=_F32) + y0_scr[j, p] for p in pairs]
        s_new = [dec_scr[j, p] * s[p] + jnp.dot(m_scr[j, p], sb[p], preferred_element_type=_F32) + n0_scr[j, p]
                 for p in pairs]
        for p in pairs:
            y_ref[0, rows, lanes[p]] = y[p]
            s_ref[0, p] = s_new[p]
        return carry

    lax.fori_loop(0, n_chunks, advance, 0)


def _rwkv_scan(r, k, v, logd, kk, beta, s0, *, reverse, block_t):
    b, t, d = r.shape
    assert t % block_t == 0 and block_t % SCAN_CHUNK == 0 and d % LANES == 0 and 2 * HEAD == LANES
    nt = t // block_t
    tmap = (lambda bi, ti: (bi, nt - 1 - ti, 0)) if reverse else (lambda bi, ti: (bi, ti, 0))
    tok = pl.BlockSpec((1, block_t, d), tmap)
    st = pl.BlockSpec((1, d // LANES, LANES, LANES), lambda bi, ti: (bi, 0, 0, 0))
    n_chunks, n_pairs = block_t // SCAN_CHUNK, d // LANES
    return pl.pallas_call(
        functools.partial(_scan_kernel, reverse=reverse, n_chunks=n_chunks),
        grid=(b, nt),
        in_specs=[tok] * 6 + [st],
        out_specs=[tok, st],
        out_shape=[jax.ShapeDtypeStruct((b, t, d), _F32),
                   jax.ShapeDtypeStruct((b, d // LANES, LANES, LANES), _F32)],
        scratch_shapes=[pltpu.VMEM((n_chunks, n_pairs, SCAN_CHUNK, LANES), _BF16),
                        pltpu.VMEM((n_chunks, n_pairs, SCAN_CHUNK, LANES), _F32),
                        pltpu.VMEM((n_chunks, n_pairs, LANES, LANES), _BF16),
                        pltpu.VMEM((n_chunks, n_pairs, LANES, LANES), _F32),
                        pltpu.VMEM((n_chunks, n_pairs, LANES, LANES), _F32)],
        compiler_params=_cparams(("parallel", "arbitrary")),
        name="rwkv_scan_rev" if reverse else "rwkv_scan_fwd",
    )(r, k, v, logd, kk, beta, s0)


def _layer_norm(x, eps=LN_EPS):
    mu = jnp.mean(x, axis=-1, keepdims=True)
    xc = x - mu
    var = jnp.mean(xc * xc, axis=-1, keepdims=True)
    return xc * lax.rsqrt(var + eps)


def _sigmoid(x):
    return 0.5 * (1.0 + jnp.tanh(0.5 * x))


def _head_sum(x):
    row = lax.broadcasted_iota(jnp.int32, (LANES, LANES), 0) // HEAD
    col = lax.broadcasted_iota(jnp.int32, (LANES, LANES), 1) // HEAD
    ones = (row == col).astype(_BF16)
    hi = x.astype(_BF16)
    lo = (x - hi.astype(_F32)).astype(_BF16)
    slabs = []
    for s in range(0, x.shape[-1], LANES):
        slabs.append(jnp.dot(hi[:, s:s + LANES], ones, preferred_element_type=_F32)
                     + jnp.dot(lo[:, s:s + LANES], ones, preferred_element_type=_F32))
    return jnp.concatenate(slabs, axis=-1)


def _mod_map(n_mod):
    return (lambda bi, ti: (bi, 0, 0)) if n_mod > 1 else (lambda bi, ti: (0, 0, 0))


def _vec_spec(d):
    return pl.BlockSpec((1, d), lambda bi, ti: (0, 0))


def _full_spec(shape):
    return pl.BlockSpec(shape, lambda bi, ti: (0,) * len(shape))


def _tok_spec(tm, d):
    return pl.BlockSpec((1, tm, d), lambda bi, ti: (bi, ti, 0))


def _ada_kernel(c_ref, w_ref, b_ref, o_ref):
    c = c_ref[...]
    o_ref[0] = _bdot(c * _sigmoid(c), w_ref[0]) + b_ref[0]


def _ada_modulation(cc, ada_w, ada_b, *, block_n):
    depth, d, n = ada_w.shape
    rows = cc.shape[0]
    return pl.pallas_call(
        _ada_kernel,
        grid=(depth, n // block_n),
        in_specs=[pl.BlockSpec((rows, d), lambda li, ni: (0, 0)),
                  pl.BlockSpec((1, d, block_n), lambda li, ni: (li, 0, ni)),
                  pl.BlockSpec((1, 1, block_n), lambda li, ni: (li, 0, ni))],
        out_specs=pl.BlockSpec((1, rows, block_n), lambda li, ni: (li, 0, ni)),
        out_shape=jax.ShapeDtypeStruct((depth, rows, n), _F32),
        compiler_params=_cparams(("parallel", "parallel")),
        name="ada_modulation",
    )(cc, ada_w, ada_b.reshape(depth, 1, n))


def _ln_mod_matmul_kernel(x_ref, sh_ref, sc_ref, w_ref, o_ref):
    h = _layer_norm(x_ref[0]) * (1.0 + sc_ref[0]) + sh_ref[0]
    o_ref[0] = _bdot(h, w_ref[...])


def _ln_mod_matmul(x, shift, scale, w, *, block_t):
    b, t, d = x.shape
    n = w.shape[1]
    mod = pl.BlockSpec((1, 1, d), _mod_map(shift.shape[0]))
    return pl.pallas_call(
        _ln_mod_matmul_kernel,
        grid=(b, t // block_t),
        in_specs=[_tok_spec(block_t, d), mod, mod, _full_spec((d, n))],
        out_specs=_tok_spec(block_t, n),
        out_shape=jax.ShapeDtypeStruct((b, t, n), _F32),
        compiler_params=_cparams(("parallel", "parallel")),
        name="ln_mod_matmul",
    )(x, shift, scale, w)


def _prep_kernel(z_ref, zp_ref, zn_ref, conv_ref, w0_ref, w2f_ref, w2b_ref, a0_ref, a2f_ref, a2b_ref,
                 g2_ref, kk_ref, ka_ref,
                 r_o, v_o, kkn_o, kf_o, kb_o, ldf_o, ldb_o, bf_o, bb_o, g_o):
    z = z_ref[0]
    tm = z.shape[0]
    d = r_o.shape[-1]
    row = lax.broadcasted_iota(jnp.int32, z.shape, 0)
    t_idx = pl.program_id(1)
    before = jnp.where(t_idx == 0, 0.0, zp_ref[0, 7:8, :])
    after = jnp.where(t_idx == pl.num_programs(1) - 1, 0.0, zn_ref[0, 0:1, :])
    up = jnp.where(row == 0, before, pltpu.roll(z, 1, 0))
    dn = jnp.where(row == tm - 1, after, pltpu.roll(z, tm - 1, 0))
    conv = conv_ref[...]
    zs = up * conv[0:1] + z * conv[1:2] + dn * conv[2:3]
    r = zs[:, 0:d]
    k = zs[:, d:2 * d]
    v = zs[:, 2 * d:3 * d]
    tw = jnp.tanh(zs[:, 3 * d:3 * d + LANES])
    la = zs[:, 3 * d + LANES:3 * d + 2 * LANES]
    dg = zs[:, 3 * d + 2 * LANES:3 * d + 3 * LANES]

    def log_decay(w0, w2_ref):
        return -math.exp(-0.5) * _sigmoid(w0 + _bdot(tw, w2_ref[...]))

    w0 = w0_ref[...]
    a0 = a0_ref[...]
    a_f = _sigmoid(a0[0:1] + _bdot(la, a2f_ref[...]))
    a_b = _sigmoid(a0[1:2] + _bdot(la, a2b_ref[...]))
    kkr = k * kk_ref[...]
    kkn = kkr * lax.rsqrt(jnp.maximum(_head_sum(kkr * kkr), 1e-12))
    ka = ka_ref[...]
    r_o[0] = r.astype(_BF16)
    v_o[0] = v.astype(_BF16)
    kkn_o[0] = kkn.astype(_BF16)
    kf_o[0] = (k * (1.0 + (a_f - 1.0) * ka)).astype(_BF16)
    kb_o[0] = (k * (1.0 + (a_b - 1.0) * ka)).astype(_BF16)
    ldf_o[0] = log_decay(w0[0:1], w2f_ref)
    ldb_o[0] = log_decay(w0[1:2], w2b_ref)
    bf_o[0] = (kkn * a_f).astype(_BF16)
    bb_o[0] = (kkn * a_b).astype(_BF16)
    g_o[0] = _bdot(_sigmoid(dg), g2_ref[...]).astype(_BF16)


def _rwkv_prep(z, conv, w0, w2f, w2b, a0, a2f, a2b, g2, k_k, k_a, *, d, block_t):
    b, t, nr = z.shape
    nb = t // block_t
    tiles = block_t // SUBLANES
    prev_tile = pl.BlockSpec((1, SUBLANES, nr), lambda bi, ti: (bi, jnp.maximum(ti * tiles - 1, 0), 0))
    next_tile = pl.BlockSpec((1, SUBLANES, nr), lambda bi, ti: (bi, jnp.minimum((ti + 1) * tiles, nb * tiles - 1), 0))
    f32, b16 = jax.ShapeDtypeStruct((b, t, d), _F32), jax.ShapeDtypeStruct((b, t, d), _BF16)
    return pl.pallas_call(
        _prep_kernel,
        grid=(b, nb),
        in_specs=[_tok_spec(block_t, nr), prev_tile, next_tile, _full_spec(conv.shape), _full_spec(w0.shape),
                  _full_spec(w2f.shape), _full_spec(w2b.shape), _full_spec(a0.shape), _full_spec(a2f.shape),
                  _full_spec(a2b.shape), _full_spec(g2.shape), _vec_spec(d), _vec_spec(d)],
        out_specs=[_tok_spec(block_t, d)] * 10,
        out_shape=[b16, b16, b16, b16, b16, f32, f32, b16, b16, b16],
        compiler_params=_cparams(("parallel", "parallel")),
        name="rwkv_prep",
    )(z, z, z, conv, w0, w2f, w2b, a0, a2f, a2b, g2, k_k, k_a)


def _sgu_kernel(z_ref, lg_ref, lb_ref, w_ref, bias_ref, o_ref, *, chunk):
    z = z_ref[0]
    ge = 0.5 * z * (1.0 + lax.erf(z * (2.0 ** -0.5)))
    d = o_ref.shape[-1]
    u = ge[:, :d]
    v = _layer_norm(ge[:, d:]) * lg_ref[...] + lb_ref[...]
    bias = bias_ref[...]
    for c in range(z.shape[0] // chunk):
        rows = slice(c * chunk, (c + 1) * chunk)
        for g in range(w_ref.shape[0]):
            cols = slice(g * (d // w_ref.shape[0]), (g + 1) * (d // w_ref.shape[0]))
            mixed = _bdot(w_ref[g], v[rows, cols]) + bias[:, cols]
            o_ref[0, rows, cols] = (u[rows, cols] * mixed).astype(o_ref.dtype)


def _spatial_gating(z_sgu, ln_g, ln_b, w, bias_full, *, block_t):
    b, t, d2 = z_sgu.shape
    d = d2 // 2
    chunk = w.shape[-1]
    return pl.pallas_call(
        functools.partial(_sgu_kernel, chunk=chunk),
        grid=(b, t // block_t),
        in_specs=[_tok_spec(block_t, d2), _vec_spec(d), _vec_spec(d), _full_spec(w.shape),
                  _full_spec(bias_full.shape)],
        out_specs=_tok_spec(block_t, d),
        out_shape=jax.ShapeDtypeStruct((b, t, d), _BF16),
        compiler_params=_cparams(("parallel", "parallel")),
        name="spatial_gating",
    )(z_sgu, ln_g, ln_b, w, bias_full)


def _merge_kernel(yf_ref, yb_ref, r_ref, kf_ref, v_ref, g_ref, ys_ref, zg_ref, x_ref, gt_ref,
                  lnxg_ref, lnxb_ref, rk_ref, wa_ref, wb_ref, wo_ref, l1g_ref, l1b_ref, o_ref, *, alpha):
    d = o_ref.shape[-1]
    y = yf_ref[0] + yb_ref[0]
    yc = y - _head_sum(y) * (1.0 / HEAD)
    var = _head_sum(yc * yc) * (1.0 / HEAD)
    yn = yc * lax.rsqrt(var + GN_EPS) * lnxg_ref[...] + lnxb_ref[...]
    f32 = lambda ref: ref[0].astype(_F32)
    bonus = _head_sum(f32(r_ref) * f32(kf_ref) * rk_ref[...]) * f32(v_ref)
    y_a = (yn + bonus) * f32(g_ref)
    zg = zg_ref[0]
    merged = _sigmoid(zg[:, :d]) * _bdot(y_a, wa_ref[...]) + _sigmoid(zg[:, d:]) * _bdot(ys_ref[0], wb_ref[...])
    out = _bdot(merged, wo_ref[...])
    o_ref[0] = _layer_norm(alpha * x_ref[0] + gt_ref[0] * out) * l1g_ref[...] + l1b_ref[...]


def _merge(yf, yb, r, kf, v, g, ys, zg, x, gate, lnx_g, lnx_b, r_k, wa, wb, wo, l1g, l1b, *, alpha, block_t):
    b, t, d = x.shape
    tok = _tok_spec(block_t, d)
    vec = _vec_spec(d)
    mat = _full_spec((d, d))
    return pl.pallas_call(
        functools.partial(_merge_kernel, alpha=alpha),
        grid=(b, t // block_t),
        in_specs=[tok] * 7 + [_tok_spec(block_t, 2 * d), tok, pl.BlockSpec((1, 1, d), _mod_map(gate.shape[0])),
                  vec, vec, vec, mat, mat, mat, vec, vec],
        out_specs=tok,
        out_shape=jax.ShapeDtypeStruct((b, t, d), _F32),
        compiler_params=_cparams(("parallel", "parallel")),
        name="branch_merge",
    )(yf, yb, r, kf, v, g, ys, zg, x, gate, lnx_g, lnx_b, r_k, wa, wb, wo, l1g, l1b)


def _router_kernel(x_ref, sh_ref, sc_ref, rw_ref, h_ref, aff_ref, *, n_exp):
    h = _layer_norm(x_ref[0]) * (1.0 + sc_ref[0]) + sh_ref[0]
    h_ref[0] = h.astype(_BF16)
    logits = jnp.dot(h, rw_ref[...], precision=lax.Precision.HIGHEST, preferred_element_type=_F32)
    lane = lax.broadcasted_iota(jnp.int32, logits.shape, 1)
    logits = jnp.where(lane < n_exp, logits, -jnp.inf)
    e = jnp.exp(logits - jnp.max(logits, axis=-1, keepdims=True))
    aff_ref[0] = e / jnp.sum(e, axis=-1, keepdims=True)


def _router(x, shift, scale, rw_pad, *, n_exp, block_t):
    b, t, d = x.shape
    mod = pl.BlockSpec((1, 1, d), _mod_map(shift.shape[0]))
    return pl.pallas_call(
        functools.partial(_router_kernel, n_exp=n_exp),
        grid=(b, t // block_t),
        in_specs=[_tok_spec(block_t, d), mod, mod, _full_spec(rw_pad.shape)],
        out_specs=[_tok_spec(block_t, d), _tok_spec(block_t, LANES)],
        out_shape=[jax.ShapeDtypeStruct((b, t, d), _BF16), jax.ShapeDtypeStruct((b, t, LANES), _F32)],
        compiler_params=_cparams(("parallel", "parallel")),
        name="moe_router",
    )(x, shift, scale, rw_pad)


def _select_kernel(aff_ref, before_ref, slot_ref, *, cap):
    a = aff_ref[0]
    key = lax.bitcast_convert_type(a, jnp.int32)
    thr = jnp.zeros((a.shape[0], 1), jnp.int32)
    for bit in range(30, -1, -1):
        cand = thr | (1 << bit)
        count = jnp.sum((key >= cand).astype(_F32), axis=1, keepdims=True)
        thr = jnp.where(count >= cap, cand, thr)
    above = key > thr
    tied = key == thr
    need = cap - jnp.sum(above.astype(_F32), axis=1, keepdims=True)
    before = before_ref[...]
    tied_before = jnp.dot(tied.astype(_BF16), before, preferred_element_type=_F32)
    chosen = jnp.logical_or(above, jnp.logical_and(tied, tied_before < need))
    slot = jnp.dot(chosen.astype(_BF16), before, preferred_element_type=_F32)
    slot_ref[0] = jnp.where(chosen, slot, -1.0)


def _expert_choice_slots(aff_t, *, cap):
    b, n_exp, t = aff_t.shape
    before = jnp.triu(jnp.ones((t, t), _BF16), k=1)
    return pl.pallas_call(
        functools.partial(_select_kernel, cap=cap),
        grid=(b,),
        in_specs=[pl.BlockSpec((1, n_exp, t), lambda bi: (bi, 0, 0)), pl.BlockSpec((t, t), lambda bi: (0, 0))],
        out_specs=pl.BlockSpec((1, n_exp, t), lambda bi: (bi, 0, 0)),
        out_shape=jax.ShapeDtypeStruct((b, n_exp, t), _F32),
        compiler_params=_cparams(("parallel",)),
        name="expert_choice_slots",
    )(aff_t, before)


def _moe_kernel(*refs, n_sets):
    ins, rest = refs[:3 * n_sets], refs[3 * n_sets:]
    w1_ref, w3_ref, w2_ref = rest[:3]
    outs = rest[3:3 + n_sets]
    xe_ref, ye_ref = rest[3 + n_sets:5 + n_sets]
    p_refs = rest[5 + n_sets:]
    e_idx = pl.program_id(1)
    f_idx = pl.program_id(2)
    caps = [p.shape[0] for p in p_refs]
    offs = [sum(caps[:i]) for i in range(n_sets)]

    @pl.when(jnp.logical_and(e_idx == 0, f_idx == 0))
    def _():
        for o_ref in outs:
            o_ref[...] = jnp.zeros_like(o_ref)

    @pl.when(f_idx == 0)
    def _():
        for i in range(n_sets):
            h_ref, slot_ref = ins[3 * i:3 * i + 2]
            cap, t = p_refs[i].shape
            slot = lax.broadcasted_iota(jnp.int32, (cap, t), 0).astype(_F32)
            p = (slot_ref[0, 0] == slot).astype(_BF16)
            p_refs[i][...] = p
            xe_ref[offs[i]:offs[i] + cap, :] = jnp.dot(p, h_ref[0], preferred_element_type=_F32).astype(_BF16)
        ye_ref[...] = jnp.zeros_like(ye_ref)

    xe = xe_ref[...]
    h1 = jnp.dot(xe, w1_ref[0, 0], preferred_element_type=_F32)
    h3 = jnp.dot(xe, w3_ref[0, 0], preferred_element_type=_F32)
    ye_ref[...] += _bdot(h1 * _sigmoid(h1) * h3, w2_ref[0, 0])

    @pl.when(f_idx == pl.num_programs(2) - 1)
    def _():
        for i in range(n_sets):
            acol_ref, o_ref = ins[3 * i + 2], outs[i]
            cap, t = p_refs[i].shape
            blk = min(t, 512)
            ye = ye_ref[offs[i]:offs[i] + cap, :].astype(_BF16)
            for t0 in range(0, t, blk):
                back = lax.dot_general(p_refs[i][:, t0:t0 + blk], ye, (((0,), (0,)), ((), ())),
                                       preferred_element_type=_F32)
                o_ref[0, t0:t0 + blk, :] += acol_ref[0, 0, t0:t0 + blk, :] * back


def _expert_choice_ffn(sets, w1, w3, w2, *, layer, block_f):
    _, n_exp, d, d_ff = w1.shape
    b = sets[0][0].shape[0]
    caps = [CAP_FACTOR * h.shape[1] // n_exp for h, _, _ in sets]
    in_specs, out_specs, out_shape, operands = [], [], [], []
    for h, a_row, a_col in sets:
        t = h.shape[1]
        in_specs += [pl.BlockSpec((1, t, d), lambda bi, ei, fi: (bi, 0, 0)),
                     pl.BlockSpec((1, 1, 1, t), lambda bi, ei, fi: (bi, ei, 0, 0)),
                     pl.BlockSpec((1, 1, t, 1), lambda bi, ei, fi: (bi, ei, 0, 0))]
        out_specs.append(pl.BlockSpec((1, t, d), lambda bi, ei, fi: (bi, 0, 0)))
        out_shape.append(jax.ShapeDtypeStruct((b, t, d), _F32))
        operands += [h, a_row, a_col]
    in_specs += [pl.BlockSpec((1, 1, d, block_f), lambda bi, ei, fi: (layer, ei, 0, fi)),
                 pl.BlockSpec((1, 1, d, block_f), lambda bi, ei, fi: (layer, ei, 0, fi)),
                 pl.BlockSpec((1, 1, block_f, d), lambda bi, ei, fi: (layer, ei, fi, 0))]
    return pl.pallas_call(
        functools.partial(_moe_kernel, n_sets=len(sets)),
        grid=(b, n_exp, d_ff // block_f),
        in_specs=in_specs,
        out_specs=out_specs,
        out_shape=out_shape,
        scratch_shapes=[pltpu.VMEM((sum(caps), d), _BF16), pltpu.VMEM((sum(caps), d), _F32)]
        + [pltpu.VMEM((cap, h.shape[1]), _BF16) for cap, (h, _, _) in zip(caps, sets)],
        compiler_params=_cparams(("parallel", "arbitrary", "arbitrary")),
        name="expert_choice_ffn",
    )(*operands, w1, w3, w2)


def _post_norm_kernel(x_ref, y_ref, gt_ref, g_ref, b_ref, o_ref, *, alpha):
    o_ref[0] = _layer_norm(alpha * x_ref[0] + gt_ref[0] * y_ref[0]) * g_ref[...] + b_ref[...]


def _post_norm(x, y, gate, g, bias, *, alpha, block_t):
    b, t, d = x.shape
    tok = _tok_spec(block_t, d)
    return pl.pallas_call(
        functools.partial(_post_norm_kernel, alpha=alpha),
        grid=(b, t // block_t),
        in_specs=[tok, tok, pl.BlockSpec((1, 1, d), _mod_map(gate.shape[0])), _vec_spec(d), _vec_spec(d)],
        out_specs=tok,
        out_shape=jax.ShapeDtypeStruct((b, t, d), _F32),
        compiler_params=_cparams(("parallel", "parallel")),
        name="post_norm",
    )(x, y, gate, g, bias)


def _block_t(t, want):
    return want if t % want == 0 else t


def _pad_rows(w, top):
    out = jnp.zeros((LANES, w.shape[1]), w.dtype)
    return lax.dynamic_update_slice(out, w, (top, 0))


def kernel(x, c, ctx, c_ctx, ada_w, ada_b, w_in, shift_conv, w0, w2, a0, a2, g2, k_k, k_a, r_k, lnx_g, lnx_b,
           sgu_ln_g, sgu_ln_b, sgu_w, sgu_b, w_branch_a, w_branch_b, w_out, ln1_g, ln1_b, router_w, exp_w1,
           exp_w3, exp_w2, ln2_g, ln2_b):
    bsz, seq, d = x.shape
    depth = w_in.shape[0]
    n_rwkv = shift_conv.shape[-1]
    r_w, r_a, r_g = w2.shape[2], a2.shape[2], g2.shape[1]
    n_exp = router_w.shape[-1]
    assert 2 * r_w == LANES and 2 * r_a == LANES and r_g == LANES and n_rwkv == 3 * d + 3 * LANES
    assert r_k.shape[2] == HEAD and d % LANES == 0
    alpha = (2 * depth) ** 0.25

    n_rows = -(-(bsz + 1) // 8) * 8
    cc = jnp.zeros((n_rows, d), _F32).at[:bsz].set(c).at[bsz].set(c_ctx)
    mods = _ada_modulation(cc, ada_w, ada_b, block_n=d)

    s_zero = jnp.zeros((bsz, d // LANES, LANES, LANES), _F32)
    vec = lambda p: p.reshape(1, d)

    ew1, ew3, ew2 = exp_w1.astype(_BF16), exp_w3.astype(_BF16), exp_w2.astype(_BF16)

    for l in range(depth):
        mod_x = mods[l, :bsz].reshape(bsz, 1, 6, d)
        mod_c = mods[l, bsz].reshape(1, 1, 6, d)
        lat_mod = [mod_x[:, :, i] for i in range(6)]
        ctx_mod = [mod_c[:, :, i] for i in range(6)]
        w_rwkv = w_in[l, :, :n_rwkv].astype(_BF16)
        w_sgu = w_in[l, :, n_rwkv:n_rwkv + 2 * d].astype(_BF16)
        w_gate = w_in[l, :, n_rwkv + 2 * d:].astype(_BF16)
        prep_w = dict(conv=shift_conv[l], w0=w0[l],
                      w2f=_pad_rows(w2[l, 0], 0).astype(_BF16), w2b=_pad_rows(w2[l, 1], r_w).astype(_BF16),
                      a0=a0[l], a2f=_pad_rows(a2[l, 0], 0).astype(_BF16),
                      a2b=_pad_rows(a2[l, 1], r_a).astype(_BF16), g2=g2[l].astype(_BF16),
                      k_k=vec(k_k[l]), k_a=vec(k_a[l]))
        chunk = sgu_w.shape[-1]
        sgu_bias = jnp.repeat(sgu_b[l].T, d // sgu_w.shape[1], axis=1)
        mix_w = (vec(lnx_g[l]), vec(lnx_b[l]), r_k[l].reshape(1, d), w_branch_a[l].astype(_BF16),
                 w_branch_b[l].astype(_BF16), w_out[l].astype(_BF16), vec(ln1_g[l]), vec(ln1_b[l]))
        rw_pad = jnp.zeros((d, LANES), _F32).at[:, :n_exp].set(router_w[l])

        def mixer(tok, mod, s0_f, s0_b, full):
            t = tok.shape[1]
            bt = _block_t(t, 512)
            z_r = _ln_mod_matmul(tok, mod[0], mod[1], w_rwkv, block_t=bt)
            r, v, kk, kf, kb, ldf, ldb, bef, beb, g = _rwkv_prep(z_r, d=d, block_t=_block_t(t, 256), **prep_w)
            y_f, s_f = _rwkv_scan(r, kf, v, ldf, kk, bef, s0_f, reverse=False, block_t=_block_t(t, 256))
            y_b, s_b = _rwkv_scan(r, kb, v, ldb, kk, beb, s0_b, reverse=True, block_t=_block_t(t, 256))
            if not full:
                return None, s_f, s_b
            z_s = _ln_mod_matmul(tok, mod[0], mod[1], w_sgu, block_t=bt)
            z_g = _ln_mod_matmul(tok, mod[0], mod[1], w_gate, block_t=bt)
            y_s = _spatial_gating(z_s, vec(sgu_ln_g[l]), vec(sgu_ln_b[l]), sgu_w[l].astype(_BF16), sgu_bias,
                                  block_t=_block_t(t, 2 * chunk))
            out = _merge(y_f, y_b, r, kf, v, g, y_s, z_g, tok, mod[2], *mix_w, alpha=alpha,
                         block_t=_block_t(t, 256))
            return out, s_f, s_b

        def moe(streams):
            sets = []
            for tok, mod in streams:
                h, aff = _router(tok, mod[3], mod[4], rw_pad, n_exp=n_exp, block_t=_block_t(tok.shape[1], 512))
                aff_t = jnp.swapaxes(aff[:, :, :n_exp], 1, 2)
                slots = _expert_choice_slots(aff_t, cap=CAP_FACTOR * tok.shape[1] // n_exp)
                sets.append((h, slots[:, :, None, :], aff_t[:, :, :, None]))
            ys = _expert_choice_ffn(sets, ew1, ew3, ew2, layer=l, block_f=_block_t(ew1.shape[-1], 1024))
            return [_post_norm(tok, y, mod[5], vec(ln2_g[l]), vec(ln2_b[l]), alpha=alpha,
                               block_t=_block_t(tok.shape[1], 512)) for (tok, mod), y in zip(streams, ys)]

        last = l == depth - 1
        ctx_mixed, s_f, s_b = mixer(ctx, ctx_mod, s_zero, s_zero, full=not last)
        x, _, _ = mixer(x, lat_mod, s_f, s_b, full=True)
        if last:
            x, = moe([(x, lat_mod)])
        else:
            x, ctx = moe([(x, lat_mod), (ctx_mixed, ctx_mod)])
    return x
```

```python
import functools
import math

import jax
import jax.numpy as jnp
from jax import lax
from jax.experimental import pallas as pl
from jax.experimental.pallas import tpu as pltpu

LANES = 128
SUBLANES = 8
HEAD = 64
SCAN_CHUNK = 64
INV_BLOCK = 16
CAP_FACTOR = 2
LN_EPS = 1e-5
GN_EPS = 64e-5
VMEM_LIMIT = 56 * 1024 * 1024

_F32 = jnp.float32
_BF16 = jnp.bfloat16


def _cparams(sem):
    return pltpu.CompilerParams(dimension_semantics=sem, vmem_limit_bytes=VMEM_LIMIT)


def _bdot(a, b):
    return jnp.dot(a.astype(_BF16), b.astype(_BF16), preferred_element_type=_F32)


def _stack_heads(x, lane_lo):
    return jnp.concatenate([jnp.where(lane_lo, x, 0.0), jnp.where(lane_lo, 0.0, x)], axis=0)


def _pdot(a, b):
    bb = b.astype(_BF16)
    b0, b1 = bb[:, :LANES], bb[:, LANES:]
    z = jnp.zeros_like(b0)
    rhs = jnp.concatenate([jnp.concatenate([b0, z], axis=1), jnp.concatenate([z, b1], axis=1)], axis=0)
    return jnp.dot(a.astype(_BF16), rhs, preferred_element_type=_F32)


def _unit_tri_inverses(mats):
    n = mats[0].shape[0]
    row = lax.broadcasted_iota(jnp.int32, (n, 2 * n), 0)
    col = lax.broadcasted_iota(jnp.int32, (n, 2 * n), 1) % n
    eye = (row == col).astype(_F32)
    same = row // INV_BLOCK == col // INV_BLOCK
    p = [jnp.where(same, -a, 0.0) for a in mats]
    t = [eye + x for x in p]
    for _ in range(3):
        p = [_pdot(x, x) for x in p]
        t = [ti + _pdot(ti, x) for ti, x in zip(t, p)]
    e = [_pdot(ti, jnp.where(same, 0.0, a)) for ti, a in zip(t, mats)]
    e2 = [_pdot(x, x) for x in e]
    f = [eye - x + x2 - _pdot(x, x2) for x, x2 in zip(e, e2)]
    return [_pdot(fi, ti) for fi, ti in zip(f, t)]


def _scan_kernel(r_ref, k_ref, v_ref, ld_ref, kk_ref, be_ref, s0_ref, y_ref, s_ref,
                 qm_scr, y0_scr, n0_scr, dec_scr, *, reverse, n_chunks):
    t_idx = pl.program_id(1)

    @pl.when(t_idx == 0)
    def _():
        s_ref[...] = s0_ref[...]

    c = SCAN_CHUNK
    pairs = range(r_ref.shape[-1] // LANES)
    lanes = [slice(p * LANES, (p + 1) * LANES) for p in pairs]
    ri = lax.broadcasted_iota(jnp.int32, (c, c), 0)
    ci = lax.broadcasted_iota(jnp.int32, (c, c), 1)
    tri = (ci >= ri if reverse else ci <= ri).astype(_F32)
    lane_lo = lax.broadcasted_iota(jnp.int32, (c, LANES), 1) < HEAD
    r2 = lax.broadcasted_iota(jnp.int32, (2 * c, 2 * c), 0)
    c2 = lax.broadcasted_iota(jnp.int32, (2 * c, 2 * c), 1)
    eye = (r2 == c2).astype(_F32)
    strict = (c2 % c > r2 % c) if reverse else (c2 % c < r2 % c)
    read_mask = strict if reverse else (c2 % c <= r2 % c)
    stack = lambda x: _stack_heads(x, lane_lo)
    contract_rows = (((0,), (0,)), ((), ()))

    group = 2 if n_chunks % 2 == 0 else 1
    pack = lambda xs: [jnp.concatenate([xs[i], xs[i + 1]], axis=1) for i in range(0, len(xs), 2)]
    unpack = lambda xs: [h for x in xs for h in (x[:, :LANES], x[:, LANES:])]

    def chunk_terms(g, carry):
        js = [g * group + dj for dj in range(group)]
        rows_of = [pl.ds(pl.multiple_of(j * c, c), c) for j in js]
        c_all = [jnp.dot(tri, ld_ref[0, rows, :], precision=lax.Precision.HIGHEST, preferred_element_type=_F32)
                 for rows in rows_of]
        units = [(dj, p) for dj in range(group) for p in pairs]
        load = lambda ref: [ref[0, rows_of[dj], lanes[p]].astype(_F32) for dj, p in units]
        c_in = [c_all[dj][:, lanes[p]] for dj, p in units]
        tot = [x[0:1, :] if reverse else x[c - 1:c, :] for x in c_in]
        e_ex = [jnp.exp(x - l) for x, l in zip(c_in, load(ld_ref))]
        e_rd = e_ex if reverse else [jnp.exp(x) for x in c_in]
        e_ninv = [jnp.exp(-x) for x in c_in]
        e_tot = [jnp.exp(t - x) for t, x in zip(tot, c_in)]
        kx = load(k_ref)
        be = load(be_ref)
        kap = [stack(x * e) for x, e in zip(load(kk_ref), e_ex)]
        rr = [stack(x * e) for x, e in zip(load(r_ref), e_rd)]
        lhs = [jnp.concatenate([a, b], axis=0).astype(_BF16) for a, b in zip(kap, rr)]
        rhs = [jnp.concatenate([stack(k * e), stack(b * e)], axis=0).astype(_BF16)
               for k, b, e in zip(kx, be, e_ninv)]
        aa = [lax.dot_general(a, b, (((1,), (1,)), ((), ())), preferred_element_type=_F32)
              for a, b in zip(lhs, rhs)]
        tf = unpack(_unit_tri_inverses(pack([jnp.where(strict, x[:2 * c, 2 * c:], 0.0) for x in aa])))
        vm = [stack(x) for x in load(v_ref)]
        a_v = [jnp.concatenate([jnp.where(strict, x[:2 * c, :2 * c], 0.0),
                                jnp.where(read_mask, x[2 * c:, :2 * c], 0.0)], axis=0) for x in aa]
        av = unpack([_pdot(a, v) for a, v in zip(pack(a_v), pack(vm))])
        pu = [_bdot(t, jnp.concatenate([a, b[:2 * c]], axis=1)).astype(_BF16) for t, a, b in zip(tf, kap, av)]
        arb = [_bdot(jnp.where(read_mask, x[2 * c:, 2 * c:], 0.0), z) for x, z in zip(aa, pu)]
        bhat = [stack(b * e).astype(_BF16) for b, e in zip(be, e_tot)]
        khat = [stack(k * e).astype(_BF16) for k, e in zip(kx, e_tot)]
        bp = [lax.dot_general(b, z, contract_rows, preferred_element_type=_F32) for b, z in zip(bhat, pu)]
        kv = [lax.dot_general(k, v.astype(_BF16), contract_rows, preferred_element_type=_F32)
              for k, v in zip(khat, vm)]
        for u, (dj, p) in enumerate(units):
            j = js[dj]
            q = rr[u] - arb[u][:, :2 * c]
            y0 = av[u][2 * c:] - arb[u][:, 2 * c:]
            half = slice((p % 2) * LANES, (p % 2 + 1) * LANES)
            qm_scr[j, p // 2, :c, half] = (q[:c] + q[c:]).astype(_BF16)
            qm_scr[j, p // 2, c:, half] = (-bp[u][:, :2 * c]).astype(_BF16)
            y0_scr[j, p // 2, :, half] = y0[:c] + y0[c:]
            n0_scr[j, p // 2, :, half] = kv[u] - bp[u][:, 2 * c:]
            dec = jnp.sum(jnp.where(eye > 0, jnp.broadcast_to(jnp.exp(tot[u]), (2 * c, LANES)), 0.0),
                          axis=1, keepdims=True)
            dec_scr[j, p // 2, :, half] = jnp.broadcast_to(dec, (2 * c, LANES))
        return carry

    lax.fori_loop(0, n_chunks // group, chunk_terms, 0)

    def advance(ic, carry):
        j = (n_chunks - 1 - ic) if reverse else ic
        rows = pl.ds(pl.multiple_of(j * c, c), c)
        s = pack([s_ref[0, p] for p in pairs])
        qm = [_pdot(qm_scr[j, i], s[i]) for i in range(len(s))]
        for i in range(len(s)):
            y_ref[0, rows, 2 * i * LANES:2 * (i + 1) * LANES] = qm[i][:c] + y0_scr[j, i]
            s_new = dec_scr[j, i] * s[i] + qm[i][c:] + n0_scr[j, i]
            s_ref[0, 2 * i] = s_new[:, :LANES]
            s_ref[0, 2 * i + 1] = s_new[:, LANES:]
        return carry

    lax.fori_loop(0, n_chunks, advance, 0)


def _rwkv_scan(r, k, v, logd, kk, beta, s0, *, reverse, block_t):
    b, t, d = r.shape
    assert t % block_t == 0 and block_t % SCAN_CHUNK == 0 and d % (2 * LANES) == 0 and 2 * HEAD == LANES
    nt = t // block_t
    tmap = (lambda bi, ti: (bi, nt - 1 - ti, 0)) if reverse else (lambda bi, ti: (bi, ti, 0))
    tok = pl.BlockSpec((1, block_t, d), tmap)
    st = pl.BlockSpec((1, d // LANES, LANES, LANES), lambda bi, ti: (bi, 0, 0, 0))
    n_chunks, n_pairs = block_t // SCAN_CHUNK, d // LANES
    return pl.pallas_call(
        functools.partial(_scan_kernel, reverse=reverse, n_chunks=n_chunks),
        grid=(b, nt),
        in_specs=[tok] * 6 + [st],
        out_specs=[tok, st],
        out_shape=[jax.ShapeDtypeStruct((b, t, d), _F32),
                   jax.ShapeDtypeStruct((b, d // LANES, LANES, LANES), _F32)],
        scratch_shapes=[pltpu.VMEM((n_chunks, n_pairs // 2, SCAN_CHUNK + LANES, 2 * LANES), _BF16),
                        pltpu.VMEM((n_chunks, n_pairs // 2, SCAN_CHUNK, 2 * LANES), _F32),
                        pltpu.VMEM((n_chunks, n_pairs // 2, LANES, 2 * LANES), _F32),
                        pltpu.VMEM((n_chunks, n_pairs // 2, LANES, 2 * LANES), _F32)],
        compiler_params=_cparams(("parallel", "arbitrary")),
        name="rwkv_scan_rev" if reverse else "rwkv_scan_fwd",
    )(r, k, v, logd, kk, beta, s0)


def _layer_norm(x, eps=LN_EPS):
    mu = jnp.mean(x, axis=-1, keepdims=True)
    xc = x - mu
    var = jnp.mean(xc * xc, axis=-1, keepdims=True)
    return xc * lax.rsqrt(var + eps)


def _sigmoid(x):
    return 0.5 * (1.0 + jnp.tanh(0.5 * x))


def _head_sum(x):
    row = lax.broadcasted_iota(jnp.int32, (LANES, LANES), 0) // HEAD
    col = lax.broadcasted_iota(jnp.int32, (LANES, LANES), 1) // HEAD
    ones = (row == col).astype(_BF16)
    hi = x.astype(_BF16)
    lo = (x - hi.astype(_F32)).astype(_BF16)
    slabs = []
    for s in range(0, x.shape[-1], LANES):
        slabs.append(jnp.dot(hi[:, s:s + LANES], ones, preferred_element_type=_F32)
                     + jnp.dot(lo[:, s:s + LANES], ones, preferred_element_type=_F32))
    return jnp.concatenate(slabs, axis=-1)


def _mod_map(n_mod):
    return (lambda bi, ti: (bi, 0, 0)) if n_mod > 1 else (lambda bi, ti: (0, 0, 0))


def _vec_spec(d):
    return pl.BlockSpec((1, d), lambda bi, ti: (0, 0))


def _full_spec(shape):
    return pl.BlockSpec(shape, lambda bi, ti: (0,) * len(shape))


def _tok_spec(tm, d):
    return pl.BlockSpec((1, tm, d), lambda bi, ti: (bi, ti, 0))


def _ada_kernel(c_ref, w_ref, b_ref, o_ref):
    c = c_ref[...]
    o_ref[0] = _bdot(c * _sigmoid(c), w_ref[0]) + b_ref[0]


def _ada_modulation(cc, ada_w, ada_b, *, block_n):
    depth, d, n = ada_w.shape
    rows = cc.shape[0]
    return pl.pallas_call(
        _ada_kernel,
        grid=(depth, n // block_n),
        in_specs=[pl.BlockSpec((rows, d), lambda li, ni: (0, 0)),
                  pl.BlockSpec((1, d, block_n), lambda li, ni: (li, 0, ni)),
                  pl.BlockSpec((1, 1, block_n), lambda li, ni: (li, 0, ni))],
        out_specs=pl.BlockSpec((1, rows, block_n), lambda li, ni: (li, 0, ni)),
        out_shape=jax.ShapeDtypeStruct((depth, rows, n), _F32),
        compiler_params=_cparams(("parallel", "parallel")),
        name="ada_modulation",
    )(cc, ada_w, ada_b.reshape(depth, 1, n))


def _ln_mod_matmul_kernel(x_ref, sh_ref, sc_ref, w_ref, o_ref):
    h = _layer_norm(x_ref[0]) * (1.0 + sc_ref[0]) + sh_ref[0]
    o_ref[0] = _bdot(h, w_ref[...])


def _ln_mod_matmul(x, shift, scale, w, *, block_t):
    b, t, d = x.shape
    n = w.shape[1]
    mod = pl.BlockSpec((1, 1, d), _mod_map(shift.shape[0]))
    return pl.pallas_call(
        _ln_mod_matmul_kernel,
        grid=(b, t // block_t),
        in_specs=[_tok_spec(block_t, d), mod, mod, _full_spec((d, n))],
        out_specs=_tok_spec(block_t, n),
        out_shape=jax.ShapeDtypeStruct((b, t, n), _F32),
        compiler_params=_cparams(("parallel", "parallel")),
        name="ln_mod_matmul",
    )(x, shift, scale, w)


def _prep_kernel(z_ref, zp_ref, zn_ref, conv_ref, w0_ref, w2f_ref, w2b_ref, a0_ref, a2f_ref, a2b_ref,
                 g2_ref, kk_ref, ka_ref,
                 r_o, v_o, kkn_o, kf_o, kb_o, ldf_o, ldb_o, bf_o, bb_o, g_o):
    z = z_ref[0]
    tm = z.shape[0]
    d = r_o.shape[-1]
    row = lax.broadcasted_iota(jnp.int32, z.shape, 0)
    t_idx = pl.program_id(1)
    before = jnp.where(t_idx == 0, 0.0, zp_ref[0, 7:8, :])
    after = jnp.where(t_idx == pl.num_programs(1) - 1, 0.0, zn_ref[0, 0:1, :])
    up = jnp.where(row == 0, before, pltpu.roll(z, 1, 0))
    dn = jnp.where(row == tm - 1, after, pltpu.roll(z, tm - 1, 0))
    conv = conv_ref[...]
    zs = up * conv[0:1] + z * conv[1:2] + dn * conv[2:3]
    r = zs[:, 0:d]
    k = zs[:, d:2 * d]
    v = zs[:, 2 * d:3 * d]
    tw = jnp.tanh(zs[:, 3 * d:3 * d + LANES])
    la = zs[:, 3 * d + LANES:3 * d + 2 * LANES]
    dg = zs[:, 3 * d + 2 * LANES:3 * d + 3 * LANES]

    def log_decay(w0, w2_ref):
        return -math.exp(-0.5) * _sigmoid(w0 + _bdot(tw, w2_ref[...]))

    w0 = w0_ref[...]
    a0 = a0_ref[...]
    a_f = _sigmoid(a0[0:1] + _bdot(la, a2f_ref[...]))
    a_b = _sigmoid(a0[1:2] + _bdot(la, a2b_ref[...]))
    kkr = k * kk_ref[...]
    kkn = kkr * lax.rsqrt(jnp.maximum(_head_sum(kkr * kkr), 1e-12))
    ka = ka_ref[...]
    r_o[0] = r.astype(_BF16)
    v_o[0] = v.astype(_BF16)
    kkn_o[0] = kkn.astype(_BF16)
    kf_o[0] = (k * (1.0 + (a_f - 1.0) * ka)).astype(_BF16)
    kb_o[0] = (k * (1.0 + (a_b - 1.0) * ka)).astype(_BF16)
    ldf_o[0] = log_decay(w0[0:1], w2f_ref)
    ldb_o[0] = log_decay(w0[1:2], w2b_ref)
    bf_o[0] = (kkn * a_f).astype(_BF16)
    bb_o[0] = (kkn * a_b).astype(_BF16)
    g_o[0] = _bdot(_sigmoid(dg), g2_ref[...]).astype(_BF16)


def _rwkv_prep(z, conv, w0, w2f, w2b, a0, a2f, a2b, g2, k_k, k_a, *, d, block_t):
    b, t, nr = z.shape
    nb = t // block_t
    tiles = block_t // SUBLANES
    prev_tile = pl.BlockSpec((1, SUBLANES, nr), lambda bi, ti: (bi, jnp.maximum(ti * tiles - 1, 0), 0))
    next_tile = pl.BlockSpec((1, SUBLANES, nr), lambda bi, ti: (bi, jnp.minimum((ti + 1) * tiles, nb * tiles - 1), 0))
    f32, b16 = jax.ShapeDtypeStruct((b, t, d), _F32), jax.ShapeDtypeStruct((b, t, d), _BF16)
    return pl.pallas_call(
        _prep_kernel,
        grid=(b, nb),
        in_specs=[_tok_spec(block_t, nr), prev_tile, next_tile, _full_spec(conv.shape), _full_spec(w0.shape),
                  _full_spec(w2f.shape), _full_spec(w2b.shape), _full_spec(a0.shape), _full_spec(a2f.shape),
                  _full_spec(a2b.shape), _full_spec(g2.shape), _vec_spec(d), _vec_spec(d)],
        out_specs=[_tok_spec(block_t, d)] * 10,
        out_shape=[b16, b16, b16, b16, b16, f32, f32, b16, b16, b16],
        compiler_params=_cparams(("parallel", "parallel")),
        name="rwkv_prep",
    )(z, z, z, conv, w0, w2f, w2b, a0, a2f, a2b, g2, k_k, k_a)


def _sgu_kernel(z_ref, lg_ref, lb_ref, w_ref, bias_ref, o_ref, *, chunk):
    z = z_ref[0]
    ge = 0.5 * z * (1.0 + lax.erf(z * (2.0 ** -0.5)))
    d = o_ref.shape[-1]
    u = ge[:, :d]
    v = _layer_norm(ge[:, d:]) * lg_ref[...] + lb_ref[...]
    bias = bias_ref[...]
    for c in range(z.shape[0] // chunk):
        rows = slice(c * chunk, (c + 1) * chunk)
        for g in range(w_ref.shape[0]):
            cols = slice(g * (d // w_ref.shape[0]), (g + 1) * (d // w_ref.shape[0]))
            mixed = _bdot(w_ref[g], v[rows, cols]) + bias[:, cols]
            o_ref[0, rows, cols] = (u[rows, cols] * mixed).astype(o_ref.dtype)


def _spatial_gating(z_sgu, ln_g, ln_b, w, bias_full, *, block_t):
    b, t, d2 = z_sgu.shape
    d = d2 // 2
    chunk = w.shape[-1]
    return pl.pallas_call(
        functools.partial(_sgu_kernel, chunk=chunk),
        grid=(b, t // block_t),
        in_specs=[_tok_spec(block_t, d2), _vec_spec(d), _vec_spec(d), _full_spec(w.shape),
                  _full_spec(bias_full.shape)],
        out_specs=_tok_spec(block_t, d),
        out_shape=jax.ShapeDtypeStruct((b, t, d), _BF16),
        compiler_params=_cparams(("parallel", "parallel")),
        name="spatial_gating",
    )(z_sgu, ln_g, ln_b, w, bias_full)


def _merge_kernel(yf_ref, yb_ref, r_ref, kf_ref, v_ref, g_ref, ys_ref, zg_ref, x_ref, gt_ref,
                  lnxg_ref, lnxb_ref, rk_ref, wa_ref, wb_ref, wo_ref, l1g_ref, l1b_ref, o_ref, *, alpha):
    d = o_ref.shape[-1]
    y = yf_ref[0] + yb_ref[0]
    yc = y - _head_sum(y) * (1.0 / HEAD)
    var = _head_sum(yc * yc) * (1.0 / HEAD)
    yn = yc * lax.rsqrt(var + GN_EPS) * lnxg_ref[...] + lnxb_ref[...]
    f32 = lambda ref: ref[0].astype(_F32)
    bonus = _head_sum(f32(r_ref) * f32(kf_ref) * rk_ref[...]) * f32(v_ref)
    y_a = (yn + bonus) * f32(g_ref)
    zg = zg_ref[0]
    merged = _sigmoid(zg[:, :d]) * _bdot(y_a, wa_ref[...]) + _sigmoid(zg[:, d:]) * _bdot(ys_ref[0], wb_ref[...])
    out = _bdot(merged, wo_ref[...])
    o_ref[0] = _layer_norm(alpha * x_ref[0] + gt_ref[0] * out) * l1g_ref[...] + l1b_ref[...]


def _merge(yf, yb, r, kf, v, g, ys, zg, x, gate, lnx_g, lnx_b, r_k, wa, wb, wo, l1g, l1b, *, alpha, block_t):
    b, t, d = x.shape
    tok = _tok_spec(block_t, d)
    vec = _vec_spec(d)
    mat = _full_spec((d, d))
    return pl.pallas_call(
        functools.partial(_merge_kernel, alpha=alpha),
        grid=(b, t // block_t),
        in_specs=[tok] * 7 + [_tok_spec(block_t, 2 * d), tok, pl.BlockSpec((1, 1, d), _mod_map(gate.shape[0])),
                  vec, vec, vec, mat, mat, mat, vec, vec],
        out_specs=tok,
        out_shape=jax.ShapeDtypeStruct((b, t, d), _F32),
        compiler_params=_cparams(("parallel", "parallel")),
        name="branch_merge",
    )(yf, yb, r, kf, v, g, ys, zg, x, gate, lnx_g, lnx_b, r_k, wa, wb, wo, l1g, l1b)


def _router_kernel(x_ref, sh_ref, sc_ref, rw_ref, h_ref, aff_ref, *, n_exp):
    h = _layer_norm(x_ref[0]) * (1.0 + sc_ref[0]) + sh_ref[0]
    h_ref[0] = h.astype(_BF16)
    logits = jnp.dot(h, rw_ref[...], precision=lax.Precision.HIGHEST, preferred_element_type=_F32)
    lane = lax.broadcasted_iota(jnp.int32, logits.shape, 1)
    logits = jnp.where(lane < n_exp, logits, -jnp.inf)
    e = jnp.exp(logits - jnp.max(logits, axis=-1, keepdims=True))
    aff_ref[0] = e / jnp.sum(e, axis=-1, keepdims=True)


def _router(x, shift, scale, rw_pad, *, n_exp, block_t):
    b, t, d = x.shape
    mod = pl.BlockSpec((1, 1, d), _mod_map(shift.shape[0]))
    return pl.pallas_call(
        functools.partial(_router_kernel, n_exp=n_exp),
        grid=(b, t // block_t),
        in_specs=[_tok_spec(block_t, d), mod, mod, _full_spec(rw_pad.shape)],
        out_specs=[_tok_spec(block_t, d), _tok_spec(block_t, LANES)],
        out_shape=[jax.ShapeDtypeStruct((b, t, d), _BF16), jax.ShapeDtypeStruct((b, t, LANES), _F32)],
        compiler_params=_cparams(("parallel", "parallel")),
        name="moe_router",
    )(x, shift, scale, rw_pad)


def _select_kernel(aff_ref, before_ref, slot_ref, *, cap):
    a = aff_ref[0]
    key = lax.bitcast_convert_type(a, jnp.int32)
    thr = jnp.zeros((a.shape[0], 1), jnp.int32)
    for bit in range(30, -1, -1):
        cand = thr | (1 << bit)
        count = jnp.sum((key >= cand).astype(_F32), axis=1, keepdims=True)
        thr = jnp.where(count >= cap, cand, thr)
    above = key > thr
    tied = key == thr
    need = cap - jnp.sum(above.astype(_F32), axis=1, keepdims=True)
    before = before_ref[...]
    tied_before = jnp.dot(tied.astype(_BF16), before, preferred_element_type=_F32)
    chosen = jnp.logical_or(above, jnp.logical_and(tied, tied_before < need))
    slot = jnp.dot(chosen.astype(_BF16), before, preferred_element_type=_F32)
    slot_ref[0] = jnp.where(chosen, slot, -1.0)


def _expert_choice_slots(aff_t, *, cap):
    b, n_exp, t = aff_t.shape
    before = jnp.triu(jnp.ones((t, t), _BF16), k=1)
    return pl.pallas_call(
        functools.partial(_select_kernel, cap=cap),
        grid=(b,),
        in_specs=[pl.BlockSpec((1, n_exp, t), lambda bi: (bi, 0, 0)), pl.BlockSpec((t, t), lambda bi: (0, 0))],
        out_specs=pl.BlockSpec((1, n_exp, t), lambda bi: (bi, 0, 0)),
        out_shape=jax.ShapeDtypeStruct((b, n_exp, t), _F32),
        compiler_params=_cparams(("parallel",)),
        name="expert_choice_slots",
    )(aff_t, before)


def _moe_kernel(*refs, n_sets):
    ins, rest = refs[:3 * n_sets], refs[3 * n_sets:]
    w1_ref, w3_ref, w2_ref = rest[:3]
    outs = rest[3:3 + n_sets]
    xe_ref, ye_ref = rest[3 + n_sets:5 + n_sets]
    p_refs = rest[5 + n_sets:]
    e_idx = pl.program_id(1)
    f_idx = pl.program_id(2)
    caps = [p.shape[0] for p in p_refs]
    offs = [sum(caps[:i]) for i in range(n_sets)]

    @pl.when(jnp.logical_and(e_idx == 0, f_idx == 0))
    def _():
        for o_ref in outs:
            o_ref[...] = jnp.zeros_like(o_ref)

    @pl.when(f_idx == 0)
    def _():
        for i in range(n_sets):
            h_ref, slot_ref = ins[3 * i:3 * i + 2]
            cap, t = p_refs[i].shape
            slot = lax.broadcasted_iota(jnp.int32, (cap, t), 0).astype(_F32)
            p = (slot_ref[0, 0] == slot).astype(_BF16)
            p_refs[i][...] = p
            xe_ref[offs[i]:offs[i] + cap, :] = jnp.dot(p, h_ref[0], preferred_element_type=_F32).astype(_BF16)
        ye_ref[...] = jnp.zeros_like(ye_ref)

    xe = xe_ref[...]
    h1 = jnp.dot(xe, w1_ref[0, 0], preferred_element_type=_F32)
    h3 = jnp.dot(xe, w3_ref[0, 0], preferred_element_type=_F32)
    ye_ref[...] += _bdot(h1 * _sigmoid(h1) * h3, w2_ref[0, 0])

    @pl.when(f_idx == pl.num_programs(2) - 1)
    def _():
        for i in range(n_sets):
            aff_ref, o_ref = ins[3 * i + 2], outs[i]
            cap, t = p_refs[i].shape
            blk = min(t, 512)
            ye = ye_ref[offs[i]:offs[i] + cap, :].astype(_BF16)
            this_expert = lax.broadcasted_iota(jnp.int32, (blk, LANES), 1) == e_idx
            for t0 in range(0, t, blk):
                back = lax.dot_general(p_refs[i][:, t0:t0 + blk], ye, (((0,), (0,)), ((), ())),
                                       preferred_element_type=_F32)
                gate = jnp.sum(jnp.where(this_expert, aff_ref[0, t0:t0 + blk, :], 0.0), axis=1, keepdims=True)
                o_ref[0, t0:t0 + blk, :] += gate * back


def _expert_choice_ffn(sets, w1, w3, w2, *, layer, block_f):
    _, n_exp, d, d_ff = w1.shape
    b = sets[0][0].shape[0]
    caps = [CAP_FACTOR * h.shape[1] // n_exp for h, _, _ in sets]
    in_specs, out_specs, out_shape, operands = [], [], [], []
    for h, slot, aff in sets:
        t = h.shape[1]
        in_specs += [pl.BlockSpec((1, t, d), lambda bi, ei, fi: (bi, 0, 0)),
                     pl.BlockSpec((1, 1, 1, t), lambda bi, ei, fi: (bi, ei, 0, 0)),
                     pl.BlockSpec((1, t, LANES), lambda bi, ei, fi: (bi, 0, 0))]
        out_specs.append(pl.BlockSpec((1, t, d), lambda bi, ei, fi: (bi, 0, 0)))
        out_shape.append(jax.ShapeDtypeStruct((b, t, d), _F32))
        operands += [h, slot, aff]
    in_specs += [pl.BlockSpec((1, 1, d, block_f), lambda bi, ei, fi: (layer, ei, 0, fi)),
                 pl.BlockSpec((1, 1, d, block_f), lambda bi, ei, fi: (layer, ei, 0, fi)),
                 pl.BlockSpec((1, 1, block_f, d), lambda bi, ei, fi: (layer, ei, fi, 0))]
    return pl.pallas_call(
        functools.partial(_moe_kernel, n_sets=len(sets)),
        grid=(b, n_exp, d_ff // block_f),
        in_specs=in_specs,
        out_specs=out_specs,
        out_shape=out_shape,
        scratch_shapes=[pltpu.VMEM((sum(caps), d), _BF16), pltpu.VMEM((sum(caps), d), _F32)]
        + [pltpu.VMEM((cap, h.shape[1]), _BF16) for cap, (h, _, _) in zip(caps, sets)],
        compiler_params=_cparams(("parallel", "arbitrary", "arbitrary")),
        name="expert_choice_ffn",
    )(*operands, w1, w3, w2)


def _post_norm_kernel(x_ref, y_ref, gt_ref, g_ref, b_ref, o_ref, *, alpha):
    o_ref[0] = _layer_norm(alpha * x_ref[0] + gt_ref[0] * y_ref[0]) * g_ref[...] + b_ref[...]


def _post_norm(x, y, gate, g, bias, *, alpha, block_t):
    b, t, d = x.shape
    tok = _tok_spec(block_t, d)
    return pl.pallas_call(
        functools.partial(_post_norm_kernel, alpha=alpha),
        grid=(b, t // block_t),
        in_specs=[tok, tok, pl.BlockSpec((1, 1, d), _mod_map(gate.shape[0])), _vec_spec(d), _vec_spec(d)],
        out_specs=tok,
        out_shape=jax.ShapeDtypeStruct((b, t, d), _F32),
        compiler_params=_cparams(("parallel", "parallel")),
        name="post_norm",
    )(x, y, gate, g, bias)


def _block_t(t, want):
    return want if t % want == 0 else t


def _pad_rows(w, top):
    out = jnp.zeros((LANES, w.shape[1]), w.dtype)
    return lax.dynamic_update_slice(out, w, (top, 0))


def kernel(x, c, ctx, c_ctx, ada_w, ada_b, w_in, shift_conv, w0, w2, a0, a2, g2, k_k, k_a, r_k, lnx_g, lnx_b,
           sgu_ln_g, sgu_ln_b, sgu_w, sgu_b, w_branch_a, w_branch_b, w_out, ln1_g, ln1_b, router_w, exp_w1,
           exp_w3, exp_w2, ln2_g, ln2_b):
    bsz, seq, d = x.shape
    depth = w_in.shape[0]
    n_rwkv = shift_conv.shape[-1]
    r_w, r_a, r_g = w2.shape[2], a2.shape[2], g2.shape[1]
    n_exp = router_w.shape[-1]
    assert 2 * r_w == LANES and 2 * r_a == LANES and r_g == LANES and n_rwkv == 3 * d + 3 * LANES
    assert r_k.shape[2] == HEAD and d % LANES == 0
    alpha = (2 * depth) ** 0.25

    n_rows = -(-(bsz + 1) // 8) * 8
    cc = jnp.zeros((n_rows, d), _F32).at[:bsz].set(c).at[bsz].set(c_ctx)
    mods = _ada_modulation(cc, ada_w, ada_b, block_n=d)

    s_zero = jnp.zeros((bsz, d // LANES, LANES, LANES), _F32)
    vec = lambda p: p.reshape(1, d)

    ew1, ew3, ew2 = exp_w1.astype(_BF16), exp_w3.astype(_BF16), exp_w2.astype(_BF16)

    for l in range(depth):
        mod_x = mods[l, :bsz].reshape(bsz, 1, 6, d)
        mod_c = mods[l, bsz].reshape(1, 1, 6, d)
        lat_mod = [mod_x[:, :, i] for i in range(6)]
        ctx_mod = [mod_c[:, :, i] for i in range(6)]
        w_rwkv = w_in[l, :, :n_rwkv].astype(_BF16)
        w_sgu = w_in[l, :, n_rwkv:n_rwkv + 2 * d].astype(_BF16)
        w_gate = w_in[l, :, n_rwkv + 2 * d:].astype(_BF16)
        prep_w = dict(conv=shift_conv[l], w0=w0[l],
                      w2f=_pad_rows(w2[l, 0], 0).astype(_BF16), w2b=_pad_rows(w2[l, 1], r_w).astype(_BF16),
                      a0=a0[l], a2f=_pad_rows(a2[l, 0], 0).astype(_BF16),
                      a2b=_pad_rows(a2[l, 1], r_a).astype(_BF16), g2=g2[l].astype(_BF16),
                      k_k=vec(k_k[l]), k_a=vec(k_a[l]))
        chunk = sgu_w.shape[-1]
        sgu_bias = jnp.repeat(sgu_b[l].T, d // sgu_w.shape[1], axis=1)
        mix_w = (vec(lnx_g[l]), vec(lnx_b[l]), r_k[l].reshape(1, d), w_branch_a[l].astype(_BF16),
                 w_branch_b[l].astype(_BF16), w_out[l].astype(_BF16), vec(ln1_g[l]), vec(ln1_b[l]))
        rw_pad = jnp.zeros((d, LANES), _F32).at[:, :n_exp].set(router_w[l])

        def mixer(tok, mod, s0_f, s0_b, full):
            t = tok.shape[1]
            bt = _block_t(t, 512)
            z_r = _ln_mod_matmul(tok, mod[0], mod[1], w_rwkv, block_t=bt)
            r, v, kk, kf, kb, ldf, ldb, bef, beb, g = _rwkv_prep(z_r, d=d, block_t=_block_t(t, 256), **prep_w)
            y_f, s_f = _rwkv_scan(r, kf, v, ldf, kk, bef, s0_f, reverse=False, block_t=_block_t(t, 512))
            y_b, s_b = _rwkv_scan(r, kb, v, ldb, kk, beb, s0_b, reverse=True, block_t=_block_t(t, 512))
            if not full:
                return None, s_f, s_b
            z_s = _ln_mod_matmul(tok, mod[0], mod[1], w_sgu, block_t=bt)
            z_g = _ln_mod_matmul(tok, mod[0], mod[1], w_gate, block_t=bt)
            y_s = _spatial_gating(z_s, vec(sgu_ln_g[l]), vec(sgu_ln_b[l]), sgu_w[l].astype(_BF16), sgu_bias,
                                  block_t=_block_t(t, 2 * chunk))
            out = _merge(y_f, y_b, r, kf, v, g, y_s, z_g, tok, mod[2], *mix_w, alpha=alpha,
                         block_t=_block_t(t, 256))
            return out, s_f, s_b

        def moe(streams):
            sets = []
            for tok, mod in streams:
                h, aff = _router(tok, mod[3], mod[4], rw_pad, n_exp=n_exp, block_t=_block_t(tok.shape[1], 512))
                aff_t = jnp.swapaxes(aff[:, :, :n_exp], 1, 2)
                slots = _expert_choice_slots(aff_t, cap=CAP_FACTOR * tok.shape[1] // n_exp)
                sets.append((h, slots[:, :, None, :], aff))
            ys = _expert_choice_ffn(sets, ew1, ew3, ew2, layer=l, block_f=_block_t(ew1.shape[-1], 1024))
            return [_post_norm(tok, y, mod[5], vec(ln2_g[l]), vec(ln2_b[l]), alpha=alpha,
                               block_t=_block_t(tok.shape[1], 512)) for (tok, mod), y in zip(streams, ys)]

        last = l == depth - 1
        ctx_mixed, s_f, s_b = mixer(ctx, ctx_mod, s_zero, s_zero, full=not last)
        x, _, _ = mixer(x, lat_mod, s_f, s_b, full=True)
        if last:
            x, = moe([(x, lat_mod)])
        else:
            x, ctx = moe([(x, lat_mod), (ctx_mixed, ctx_mod)])
    return x
```

```python
import functools
import math

import jax
import jax.numpy as jnp
from jax import lax
from jax.experimental import pallas as pl
from jax.experimental.pallas import tpu as pltpu

LANES = 128
SUBLANES = 8
HEAD = 64
SCAN_CHUNK = 64
INV_BLOCK = 16
CAP_FACTOR = 2
LN_EPS = 1e-5
GN_EPS = 64e-5
VMEM_LIMIT = 56 * 1024 * 1024

_F32 = jnp.float32
_BF16 = jnp.bfloat16


def _cparams(sem):
    return pltpu.CompilerParams(dimension_semantics=sem, vmem_limit_bytes=VMEM_LIMIT)


def _bdot(a, b):
    return jnp.dot(a.astype(_BF16), b.astype(_BF16), preferred_element_type=_F32)


def _stack_heads(x, lane_lo):
    return jnp.concatenate([jnp.where(lane_lo, x, 0.0), jnp.where(lane_lo, 0.0, x)], axis=0)


def _pdot(a, b):
    bb = b.astype(_BF16)
    b0, b1 = bb[:, :LANES], bb[:, LANES:]
    z = jnp.zeros_like(b0)
    rhs = jnp.concatenate([jnp.concatenate([b0, z], axis=1), jnp.concatenate([z, b1], axis=1)], axis=0)
    return jnp.dot(a.astype(_BF16), rhs, preferred_element_type=_F32)


def _unit_tri_inverses(mats):
    n = mats[0].shape[0]
    row = lax.broadcasted_iota(jnp.int32, (n, 2 * n), 0)
    col = lax.broadcasted_iota(jnp.int32, (n, 2 * n), 1) % n
    eye = (row == col).astype(_F32)
    same = row // INV_BLOCK == col // INV_BLOCK
    p = [jnp.where(same, -a, 0.0) for a in mats]
    t = [eye + x for x in p]
    for _ in range(3):
        p = [_pdot(x, x) for x in p]
        t = [ti + _pdot(ti, x) for ti, x in zip(t, p)]
    e = [_pdot(ti, jnp.where(same, 0.0, a)) for ti, a in zip(t, mats)]
    e2 = [_pdot(x, x) for x in e]
    f = [eye - x + x2 - _pdot(x, x2) for x, x2 in zip(e, e2)]
    return [_pdot(fi, ti) for fi, ti in zip(f, t)]


def _scan_kernel(r_ref, k_ref, v_ref, ld_ref, kk_ref, be_ref, s0_ref, y_ref, s_ref,
                 qm_scr, y0_scr, n0_scr, dec_scr, *, reverse, n_chunks):
    t_idx = pl.program_id(1)

    @pl.when(t_idx == 0)
    def _():
        s_ref[...] = s0_ref[...]

    c = SCAN_CHUNK
    pairs = range(r_ref.shape[-1] // LANES)
    lanes = [slice(p * LANES, (p + 1) * LANES) for p in pairs]
    ri = lax.broadcasted_iota(jnp.int32, (c, c), 0)
    ci = lax.broadcasted_iota(jnp.int32, (c, c), 1)
    tri = (ci >= ri if reverse else ci <= ri).astype(_F32)
    lane_lo = lax.broadcasted_iota(jnp.int32, (c, LANES), 1) < HEAD
    r2 = lax.broadcasted_iota(jnp.int32, (2 * c, 2 * c), 0)
    c2 = lax.broadcasted_iota(jnp.int32, (2 * c, 2 * c), 1)
    eye = (r2 == c2).astype(_F32)
    strict = (c2 % c > r2 % c) if reverse else (c2 % c < r2 % c)
    read_mask = strict if reverse else (c2 % c <= r2 % c)
    stack = lambda x: _stack_heads(x, lane_lo)
    contract_rows = (((0,), (0,)), ((), ()))

    group = 2 if n_chunks % 2 == 0 else 1
    pack = lambda xs: [jnp.concatenate([xs[i], xs[i + 1]], axis=1) for i in range(0, len(xs), 2)]
    unpack = lambda xs: [h for x in xs for h in (x[:, :LANES], x[:, LANES:])]

    def chunk_terms(g, carry):
        js = [g * group + dj for dj in range(group)]
        rows_of = [pl.ds(pl.multiple_of(j * c, c), c) for j in js]
        c_all = [jnp.dot(tri, ld_ref[0, rows, :], precision=lax.Precision.HIGHEST, preferred_element_type=_F32)
                 for rows in rows_of]
        units = [(dj, p) for dj in range(group) for p in pairs]
        load = lambda ref: [ref[0, rows_of[dj], lanes[p]].astype(_F32) for dj, p in units]
        c_in = [c_all[dj][:, lanes[p]] for dj, p in units]
        tot = [x[0:1, :] if reverse else x[c - 1:c, :] for x in c_in]
        e_ex = [jnp.exp(x - l) for x, l in zip(c_in, load(ld_ref))]
        e_rd = e_ex if reverse else [jnp.exp(x) for x in c_in]
        e_ninv = [jnp.exp(-x) for x in c_in]
        e_tot = [jnp.exp(t - x) for t, x in zip(tot, c_in)]
        kx = load(k_ref)
        be = load(be_ref)
        kap = [stack(x * e) for x, e in zip(load(kk_ref), e_ex)]
        rr = [stack(x * e) for x, e in zip(load(r_ref), e_rd)]
        lhs = [jnp.concatenate([a, b], axis=0).astype(_BF16) for a, b in zip(kap, rr)]
        rhs = [jnp.concatenate([stack(k * e), stack(b * e)], axis=0).astype(_BF16)
               for k, b, e in zip(kx, be, e_ninv)]
        aa = [lax.dot_general(a, b, (((1,), (1,)), ((), ())), preferred_element_type=_F32)
              for a, b in zip(lhs, rhs)]
        tf = unpack(_unit_tri_inverses(pack([jnp.where(strict, x[:2 * c, 2 * c:], 0.0) for x in aa])))
        vm = [stack(x) for x in load(v_ref)]
        a_v = [jnp.concatenate([jnp.where(strict, x[:2 * c, :2 * c], 0.0),
                                jnp.where(read_mask, x[2 * c:, :2 * c], 0.0)], axis=0) for x in aa]
        av = unpack([_pdot(a, v) for a, v in zip(pack(a_v), pack(vm))])
        pu = [_bdot(t, jnp.concatenate([a, b[:2 * c]], axis=1)).astype(_BF16) for t, a, b in zip(tf, kap, av)]
        arb = [_bdot(jnp.where(read_mask, x[2 * c:, 2 * c:], 0.0), z) for x, z in zip(aa, pu)]
        bhat = [stack(b * e).astype(_BF16) for b, e in zip(be, e_tot)]
        khat = [stack(k * e).astype(_BF16) for k, e in zip(kx, e_tot)]
        bp = [lax.dot_general(b, z, contract_rows, preferred_element_type=_F32) for b, z in zip(bhat, pu)]
        kv = [lax.dot_general(k, v.astype(_BF16), contract_rows, preferred_element_type=_F32)
              for k, v in zip(khat, vm)]
        for u, (dj, p) in enumerate(units):
            j = js[dj]
            q = rr[u] - arb[u][:, :2 * c]
            y0 = av[u][2 * c:] - arb[u][:, 2 * c:]
            half = slice((p % 2) * LANES, (p % 2 + 1) * LANES)
            qm_scr[j, p // 2, :c, half] = (q[:c] + q[c:]).astype(_BF16)
            qm_scr[j, p // 2, c:, half] = (-bp[u][:, :2 * c]).astype(_BF16)
            y0_scr[j, p // 2, :, half] = y0[:c] + y0[c:]
            n0_scr[j, p // 2, :, half] = kv[u] - bp[u][:, 2 * c:]
            dec = jnp.sum(jnp.where(eye > 0, jnp.broadcast_to(jnp.exp(tot[u]), (2 * c, LANES)), 0.0),
                          axis=1, keepdims=True)
            dec_scr[j, p // 2, :, half] = jnp.broadcast_to(dec, (2 * c, LANES))
        return carry

    lax.fori_loop(0, n_chunks // group, chunk_terms, 0)

    def advance(ic, carry):
        j = (n_chunks - 1 - ic) if reverse else ic
        rows = pl.ds(pl.multiple_of(j * c, c), c)
        s = pack([s_ref[0, p] for p in pairs])
        qm = [_pdot(qm_scr[j, i], s[i]) for i in range(len(s))]
        for i in range(len(s)):
            y_ref[0, rows, 2 * i * LANES:2 * (i + 1) * LANES] = qm[i][:c] + y0_scr[j, i]
            s_new = dec_scr[j, i] * s[i] + qm[i][c:] + n0_scr[j, i]
            s_ref[0, 2 * i] = s_new[:, :LANES]
            s_ref[0, 2 * i + 1] = s_new[:, LANES:]
        return carry

    lax.fori_loop(0, n_chunks, advance, 0)


def _rwkv_scan(r, k, v, logd, kk, beta, s0, *, reverse, block_t):
    b, t, d = r.shape
    assert t % block_t == 0 and block_t % SCAN_CHUNK == 0 and d % (2 * LANES) == 0 and 2 * HEAD == LANES
    nt = t // block_t
    tmap = (lambda bi, ti: (bi, nt - 1 - ti, 0)) if reverse else (lambda bi, ti: (bi, ti, 0))
    tok = pl.BlockSpec((1, block_t, d), tmap)
    st = pl.BlockSpec((1, d // LANES, LANES, LANES), lambda bi, ti: (bi, 0, 0, 0))
    n_chunks, n_pairs = block_t // SCAN_CHUNK, d // LANES
    return pl.pallas_call(
        functools.partial(_scan_kernel, reverse=reverse, n_chunks=n_chunks),
        grid=(b, nt),
        in_specs=[tok] * 6 + [st],
        out_specs=[tok, st],
        out_shape=[jax.ShapeDtypeStruct((b, t, d), _F32),
                   jax.ShapeDtypeStruct((b, d // LANES, LANES, LANES), _F32)],
        scratch_shapes=[pltpu.VMEM((n_chunks, n_pairs // 2, SCAN_CHUNK + LANES, 2 * LANES), _BF16),
                        pltpu.VMEM((n_chunks, n_pairs // 2, SCAN_CHUNK, 2 * LANES), _F32),
                        pltpu.VMEM((n_chunks, n_pairs // 2, LANES, 2 * LANES), _F32),
                        pltpu.VMEM((n_chunks, n_pairs // 2, LANES, 2 * LANES), _F32)],
        compiler_params=_cparams(("parallel", "arbitrary")),
        name="rwkv_scan_rev" if reverse else "rwkv_scan_fwd",
    )(r, k, v, logd, kk, beta, s0)


def _layer_norm(x, eps=LN_EPS):
    mu = jnp.mean(x, axis=-1, keepdims=True)
    xc = x - mu
    var = jnp.mean(xc * xc, axis=-1, keepdims=True)
    return xc * lax.rsqrt(var + eps)


def _sigmoid(x):
    return 0.5 * (1.0 + jnp.tanh(0.5 * x))


def _head_sum(x):
    row = lax.broadcasted_iota(jnp.int32, (LANES, LANES), 0) // HEAD
    col = lax.broadcasted_iota(jnp.int32, (LANES, LANES), 1) // HEAD
    ones = (row == col).astype(_BF16)
    hi = x.astype(_BF16)
    lo = (x - hi.astype(_F32)).astype(_BF16)
    slabs = []
    for s in range(0, x.shape[-1], LANES):
        slabs.append(jnp.dot(hi[:, s:s + LANES], ones, preferred_element_type=_F32)
                     + jnp.dot(lo[:, s:s + LANES], ones, preferred_element_type=_F32))
    return jnp.concatenate(slabs, axis=-1)


def _mod_map(n_mod):
    return (lambda bi, ti: (bi, 0, 0)) if n_mod > 1 else (lambda bi, ti: (0, 0, 0))


def _vec_spec(d):
    return pl.BlockSpec((1, d), lambda bi, ti: (0, 0))


def _full_spec(shape):
    return pl.BlockSpec(shape, lambda bi, ti: (0,) * len(shape))


def _tok_spec(tm, d):
    return pl.BlockSpec((1, tm, d), lambda bi, ti: (bi, ti, 0))


def _ada_kernel(c_ref, w_ref, b_ref, o_ref):
    c = c_ref[...]
    o_ref[0] = _bdot(c * _sigmoid(c), w_ref[0]) + b_ref[0]


def _ada_modulation(cc, ada_w, ada_b, *, block_n):
    depth, d, n = ada_w.shape
    rows = cc.shape[0]
    return pl.pallas_call(
        _ada_kernel,
        grid=(depth, n // block_n),
        in_specs=[pl.BlockSpec((rows, d), lambda li, ni: (0, 0)),
                  pl.BlockSpec((1, d, block_n), lambda li, ni: (li, 0, ni)),
                  pl.BlockSpec((1, 1, block_n), lambda li, ni: (li, 0, ni))],
        out_specs=pl.BlockSpec((1, rows, block_n), lambda li, ni: (li, 0, ni)),
        out_shape=jax.ShapeDtypeStruct((depth, rows, n), _F32),
        compiler_params=_cparams(("parallel", "parallel")),
        name="ada_modulation",
    )(cc, ada_w, ada_b.reshape(depth, 1, n))


def _ln_mod_matmul_kernel(x_ref, sh_ref, sc_ref, w_ref, o_ref):
    h = _layer_norm(x_ref[0]) * (1.0 + sc_ref[0]) + sh_ref[0]
    o_ref[0] = _bdot(h, w_ref[...])


def _ln_mod_matmul(x, shift, scale, w, *, block_t):
    b, t, d = x.shape
    n = w.shape[1]
    mod = pl.BlockSpec((1, 1, d), _mod_map(shift.shape[0]))
    return pl.pallas_call(
        _ln_mod_matmul_kernel,
        grid=(b, t // block_t),
        in_specs=[_tok_spec(block_t, d), mod, mod, _full_spec((d, n))],
        out_specs=_tok_spec(block_t, n),
        out_shape=jax.ShapeDtypeStruct((b, t, n), _F32),
        compiler_params=_cparams(("parallel", "parallel")),
        name="ln_mod_matmul",
    )(x, shift, scale, w)


def _prep_kernel(z_ref, zp_ref, zn_ref, conv_ref, w0_ref, w2f_ref, w2b_ref, a0_ref, a2f_ref, a2b_ref,
                 g2_ref, kk_ref, ka_ref,
                 r_o, v_o, kkn_o, kf_o, kb_o, ldf_o, ldb_o, bf_o, bb_o, g_o):
    z = z_ref[0]
    tm = z.shape[0]
    d = r_o.shape[-1]
    row = lax.broadcasted_iota(jnp.int32, z.shape, 0)
    t_idx = pl.program_id(1)
    before = jnp.where(t_idx == 0, 0.0, zp_ref[0, SUBLANES - 1:SUBLANES, :])
    after = jnp.where(t_idx == pl.num_programs(1) - 1, 0.0, zn_ref[0, 0:1, :])
    up = jnp.where(row == 0, before, pltpu.roll(z, 1, 0))
    dn = jnp.where(row == tm - 1, after, pltpu.roll(z, tm - 1, 0))
    conv = conv_ref[...]
    zs = up * conv[0:1] + z * conv[1:2] + dn * conv[2:3]
    r = zs[:, 0:d]
    k = zs[:, d:2 * d]
    v = zs[:, 2 * d:3 * d]
    tw = jnp.tanh(zs[:, 3 * d:3 * d + LANES])
    la = zs[:, 3 * d + LANES:3 * d + 2 * LANES]
    dg = zs[:, 3 * d + 2 * LANES:3 * d + 3 * LANES]

    def log_decay(w0, w2_ref):
        return -math.exp(-0.5) * _sigmoid(w0 + _bdot(tw, w2_ref[...]))

    w0 = w0_ref[...]
    a0 = a0_ref[...]
    a_f = _sigmoid(a0[0:1] + _bdot(la, a2f_ref[...]))
    a_b = _sigmoid(a0[1:2] + _bdot(la, a2b_ref[...]))
    kkr = k * kk_ref[...]
    kkn = kkr * lax.rsqrt(jnp.maximum(_head_sum(kkr * kkr), 1e-12))
    ka = ka_ref[...]
    r_o[0] = r.astype(_BF16)
    v_o[0] = v.astype(_BF16)
    kkn_o[0] = kkn.astype(_BF16)
    kf_o[0] = (k * (1.0 + (a_f - 1.0) * ka)).astype(_BF16)
    kb_o[0] = (k * (1.0 + (a_b - 1.0) * ka)).astype(_BF16)
    ldf_o[0] = log_decay(w0[0:1], w2f_ref)
    ldb_o[0] = log_decay(w0[1:2], w2b_ref)
    bf_o[0] = (kkn * a_f).astype(_BF16)
    bb_o[0] = (kkn * a_b).astype(_BF16)
    g_o[0] = _bdot(_sigmoid(dg), g2_ref[...]).astype(_BF16)


def _rwkv_prep(z, conv, w0, w2f, w2b, a0, a2f, a2b, g2, k_k, k_a, *, d, block_t):
    b, t, nr = z.shape
    nb = t // block_t
    tiles = block_t // SUBLANES
    prev_tile = pl.BlockSpec((1, SUBLANES, nr), lambda bi, ti: (bi, jnp.maximum(ti * tiles - 1, 0), 0))
    next_tile = pl.BlockSpec((1, SUBLANES, nr), lambda bi, ti: (bi, jnp.minimum((ti + 1) * tiles, nb * tiles - 1), 0))
    f32, b16 = jax.ShapeDtypeStruct((b, t, d), _F32), jax.ShapeDtypeStruct((b, t, d), _BF16)
    return pl.pallas_call(
        _prep_kernel,
        grid=(b, nb),
        in_specs=[_tok_spec(block_t, nr), prev_tile, next_tile, _full_spec(conv.shape), _full_spec(w0.shape),
                  _full_spec(w2f.shape), _full_spec(w2b.shape), _full_spec(a0.shape), _full_spec(a2f.shape),
                  _full_spec(a2b.shape), _full_spec(g2.shape), _vec_spec(d), _vec_spec(d)],
        out_specs=[_tok_spec(block_t, d)] * 10,
        out_shape=[b16, b16, b16, b16, b16, f32, f32, b16, b16, b16],
        compiler_params=_cparams(("parallel", "parallel")),
        name="rwkv_prep",
    )(z, z, z, conv, w0, w2f, w2b, a0, a2f, a2b, g2, k_k, k_a)


def _sgu_kernel(z_ref, lg_ref, lb_ref, w_ref, bias_ref, o_ref, *, chunk):
    z = z_ref[0]
    ge = 0.5 * z * (1.0 + lax.erf(z * (2.0 ** -0.5)))
    d = o_ref.shape[-1]
    u = ge[:, :d]
    v = _layer_norm(ge[:, d:]) * lg_ref[...] + lb_ref[...]
    bias = bias_ref[...]
    for c in range(z.shape[0] // chunk):
        rows = slice(c * chunk, (c + 1) * chunk)
        for g in range(w_ref.shape[0]):
            cols = slice(g * (d // w_ref.shape[0]), (g + 1) * (d // w_ref.shape[0]))
            mixed = _bdot(w_ref[g], v[rows, cols]) + bias[:, cols]
            o_ref[0, rows, cols] = (u[rows, cols] * mixed).astype(o_ref.dtype)


def _spatial_gating(z_sgu, ln_g, ln_b, w, bias_full, *, block_t):
    b, t, _ = z_sgu.shape
    d = ln_g.shape[-1]
    d2 = 2 * d
    chunk = w.shape[-1]
    return pl.pallas_call(
        functools.partial(_sgu_kernel, chunk=chunk),
        grid=(b, t // block_t),
        in_specs=[_tok_spec(block_t, d2), _vec_spec(d), _vec_spec(d), _full_spec(w.shape),
                  _full_spec(bias_full.shape)],
        out_specs=_tok_spec(block_t, d),
        out_shape=jax.ShapeDtypeStruct((b, t, d), _BF16),
        compiler_params=_cparams(("parallel", "parallel")),
        name="spatial_gating",
    )(z_sgu, ln_g, ln_b, w, bias_full)


def _merge_kernel(yf_ref, yb_ref, r_ref, kf_ref, v_ref, g_ref, ys_ref, zg_ref, x_ref, gt_ref,
                  lnxg_ref, lnxb_ref, rk_ref, wa_ref, wb_ref, wo_ref, l1g_ref, l1b_ref, o_ref, *, alpha):
    d = o_ref.shape[-1]
    y = yf_ref[0] + yb_ref[0]
    yc = y - _head_sum(y) * (1.0 / HEAD)
    var = _head_sum(yc * yc) * (1.0 / HEAD)
    yn = yc * lax.rsqrt(var + GN_EPS) * lnxg_ref[...] + lnxb_ref[...]
    f32 = lambda ref: ref[0].astype(_F32)
    bonus = _head_sum(f32(r_ref) * f32(kf_ref) * rk_ref[...]) * f32(v_ref)
    y_a = (yn + bonus) * f32(g_ref)
    zg = zg_ref[0]
    merged = _sigmoid(zg[:, :d]) * _bdot(y_a, wa_ref[...]) + _sigmoid(zg[:, d:]) * _bdot(ys_ref[0], wb_ref[...])
    out = _bdot(merged, wo_ref[...])
    o_ref[0] = _layer_norm(alpha * x_ref[0] + gt_ref[0] * out) * l1g_ref[...] + l1b_ref[...]


def _merge(yf, yb, r, kf, v, g, ys, z_sg, x, gate, lnx_g, lnx_b, r_k, wa, wb, wo, l1g, l1b, *, alpha, block_t):
    b, t, d = x.shape
    tok = _tok_spec(block_t, d)
    vec = _vec_spec(d)
    mat = _full_spec((d, d))
    gate_half = pl.BlockSpec((1, block_t, 2 * d), lambda bi, ti: (bi, ti, 1))
    return pl.pallas_call(
        functools.partial(_merge_kernel, alpha=alpha),
        grid=(b, t // block_t),
        in_specs=[tok] * 7 + [gate_half, tok, pl.BlockSpec((1, 1, d), _mod_map(gate.shape[0])),
                  vec, vec, vec, mat, mat, mat, vec, vec],
        out_specs=tok,
        out_shape=jax.ShapeDtypeStruct((b, t, d), _F32),
        compiler_params=_cparams(("parallel", "parallel")),
        name="branch_merge",
    )(yf, yb, r, kf, v, g, ys, z_sg, x, gate, lnx_g, lnx_b, r_k, wa, wb, wo, l1g, l1b)


def _split_bf16(x):
    hi = x.astype(_BF16)
    return hi, (x - hi.astype(_F32)).astype(_BF16)


def _router_kernel(x_ref, sh_ref, sc_ref, rw_ref, h_ref, aff_ref, *, n_exp):
    h = _layer_norm(x_ref[0]) * (1.0 + sc_ref[0]) + sh_ref[0]
    h_ref[0] = h.astype(_BF16)
    h_hi, h_lo = _split_bf16(h)
    w_hi, w_lo = _split_bf16(rw_ref[...])
    dot = lambda a, b: jnp.dot(a, b, preferred_element_type=_F32)
    logits = dot(h_hi, w_hi) + (dot(h_hi, w_lo) + dot(h_lo, w_hi))
    lane = lax.broadcasted_iota(jnp.int32, logits.shape, 1)
    logits = jnp.where(lane < n_exp, logits, -jnp.inf)
    e = jnp.exp(logits - jnp.max(logits, axis=-1, keepdims=True))
    aff_ref[0] = e / jnp.sum(e, axis=-1, keepdims=True)


def _router(x, shift, scale, rw_pad, *, n_exp, block_t):
    b, t, d = x.shape
    mod = pl.BlockSpec((1, 1, d), _mod_map(shift.shape[0]))
    return pl.pallas_call(
        functools.partial(_router_kernel, n_exp=n_exp),
        grid=(b, t // block_t),
        in_specs=[_tok_spec(block_t, d), mod, mod, _full_spec(rw_pad.shape)],
        out_specs=[_tok_spec(block_t, d), _tok_spec(block_t, LANES)],
        out_shape=[jax.ShapeDtypeStruct((b, t, d), _BF16), jax.ShapeDtypeStruct((b, t, LANES), _F32)],
        compiler_params=_cparams(("parallel", "parallel")),
        name="moe_router",
    )(x, shift, scale, rw_pad)


def _select_kernel(aff_ref, before_ref, slot_ref, *, cap):
    a = aff_ref[0]
    key = lax.bitcast_convert_type(a, jnp.int32)
    thr = jnp.zeros((a.shape[0], 1), jnp.int32)
    for bit in range(30, -1, -1):
        cand = thr | (1 << bit)
        count = jnp.sum((key >= cand).astype(_F32), axis=1, keepdims=True)
        thr = jnp.where(count >= cap, cand, thr)
    above = key > thr
    tied = key == thr
    need = cap - jnp.sum(above.astype(_F32), axis=1, keepdims=True)
    before = before_ref[...]
    tied_before = jnp.dot(tied.astype(_BF16), before, preferred_element_type=_F32)
    chosen = jnp.logical_or(above, jnp.logical_and(tied, tied_before < need))
    slot = jnp.dot(chosen.astype(_BF16), before, preferred_element_type=_F32)
    slot_ref[0] = jnp.where(chosen, slot, -1.0)


def _expert_choice_slots(aff_t, *, cap):
    b, n_exp, t = aff_t.shape
    before = jnp.triu(jnp.ones((t, t), _BF16), k=1)
    return pl.pallas_call(
        functools.partial(_select_kernel, cap=cap),
        grid=(b,),
        in_specs=[pl.BlockSpec((1, n_exp, t), lambda bi: (bi, 0, 0)), pl.BlockSpec((t, t), lambda bi: (0, 0))],
        out_specs=pl.BlockSpec((1, n_exp, t), lambda bi: (bi, 0, 0)),
        out_shape=jax.ShapeDtypeStruct((b, n_exp, t), _F32),
        compiler_params=_cparams(("parallel",)),
        name="expert_choice_slots",
    )(aff_t, before)


def _moe_kernel(*refs, n_sets):
    ins, rest = refs[:3 * n_sets], refs[3 * n_sets:]
    w1_ref, w3_ref, w2_ref = rest[:3]
    outs = rest[3:3 + n_sets]
    xe_ref, ye_ref = rest[3 + n_sets:5 + n_sets]
    p_refs = rest[5 + n_sets:]
    e_idx = pl.program_id(1)
    f_idx = pl.program_id(2)
    caps = [p.shape[0] for p in p_refs]
    offs = [sum(caps[:i]) for i in range(n_sets)]

    @pl.when(jnp.logical_and(e_idx == 0, f_idx == 0))
    def _():
        for o_ref in outs:
            o_ref[...] = jnp.zeros_like(o_ref)

    @pl.when(f_idx == 0)
    def _():
        for i in range(n_sets):
            h_ref, slot_ref = ins[3 * i:3 * i + 2]
            cap, t = p_refs[i].shape
            slot = lax.broadcasted_iota(jnp.int32, (cap, t), 0).astype(_F32)
            p = (slot_ref[0, 0] == slot).astype(_BF16)
            p_refs[i][...] = p
            xe_ref[offs[i]:offs[i] + cap, :] = jnp.dot(p, h_ref[0], preferred_element_type=_F32).astype(_BF16)
        ye_ref[...] = jnp.zeros_like(ye_ref)

    xe = xe_ref[...]
    h1 = jnp.dot(xe, w1_ref[0, 0], preferred_element_type=_F32)
    h3 = jnp.dot(xe, w3_ref[0, 0], preferred_element_type=_F32)
    ye_ref[...] += _bdot(h1 * _sigmoid(h1) * h3, w2_ref[0, 0])

    @pl.when(f_idx == pl.num_programs(2) - 1)
    def _():
        for i in range(n_sets):
            aff_ref, o_ref = ins[3 * i + 2], outs[i]
            cap, t = p_refs[i].shape
            blk = min(t, SCATTER_ROWS)
            ye = ye_ref[offs[i]:offs[i] + cap, :].astype(_BF16)
            this_expert = lax.broadcasted_iota(jnp.int32, (blk, LANES), 1) == e_idx
            for t0 in range(0, t, blk):
                back = lax.dot_general(p_refs[i][:, t0:t0 + blk], ye, (((0,), (0,)), ((), ())),
                                       preferred_element_type=_F32)
                gate = jnp.sum(jnp.where(this_expert, aff_ref[0, t0:t0 + blk, :], 0.0), axis=1, keepdims=True)
                o_ref[0, t0:t0 + blk, :] += gate * back


def _expert_choice_ffn(sets, w1, w3, w2, *, layer, block_f):
    _, n_exp, d, d_ff = w1.shape
    b = sets[0][0].shape[0]
    caps = [CAP_FACTOR * h.shape[1] // n_exp for h, _, _ in sets]
    in_specs, out_specs, out_shape, operands = [], [], [], []
    for h, slot, aff in sets:
        t = h.shape[1]
        in_specs += [pl.BlockSpec((1, t, d), lambda bi, ei, fi: (bi, 0, 0)),
                     pl.BlockSpec((1, 1, 1, t), lambda bi, ei, fi: (bi, ei, 0, 0)),
                     pl.BlockSpec((1, t, LANES), lambda bi, ei, fi: (bi, 0, 0))]
        out_specs.append(pl.BlockSpec((1, t, d), lambda bi, ei, fi: (bi, 0, 0)))
        out_shape.append(jax.ShapeDtypeStruct((b, t, d), _F32))
        operands += [h, slot, aff]
    in_specs += [pl.BlockSpec((1, 1, d, block_f), lambda bi, ei, fi: (layer, ei, 0, fi)),
                 pl.BlockSpec((1, 1, d, block_f), lambda bi, ei, fi: (layer, ei, 0, fi)),
                 pl.BlockSpec((1, 1, block_f, d), lambda bi, ei, fi: (layer, ei, fi, 0))]
    return pl.pallas_call(
        functools.partial(_moe_kernel, n_sets=len(sets)),
        grid=(b, n_exp, d_ff // block_f),
        in_specs=in_specs,
        out_specs=out_specs,
        out_shape=out_shape,
        scratch_shapes=[pltpu.VMEM((sum(caps), d), _BF16), pltpu.VMEM((sum(caps), d), _F32)]
        + [pltpu.VMEM((cap, h.shape[1]), _BF16) for cap, (h, _, _) in zip(caps, sets)],
        compiler_params=_cparams(("parallel", "arbitrary", "arbitrary")),
        name="expert_choice_ffn",
    )(*operands, w1, w3, w2)


def _post_norm_kernel(x_ref, y_ref, gt_ref, g_ref, b_ref, o_ref, *, alpha):
    o_ref[0] = _layer_norm(alpha * x_ref[0] + gt_ref[0] * y_ref[0]) * g_ref[...] + b_ref[...]


def _post_norm(x, y, gate, g, bias, *, alpha, block_t):
    b, t, d = x.shape
    tok = _tok_spec(block_t, d)
    return pl.pallas_call(
        functools.partial(_post_norm_kernel, alpha=alpha),
        grid=(b, t // block_t),
        in_specs=[tok, tok, pl.BlockSpec((1, 1, d), _mod_map(gate.shape[0])), _vec_spec(d), _vec_spec(d)],
        out_specs=tok,
        out_shape=jax.ShapeDtypeStruct((b, t, d), _F32),
        compiler_params=_cparams(("parallel", "parallel")),
        name="post_norm",
    )(x, y, gate, g, bias)


TOK_BLOCK = 512
TOK_BLOCK_WIDE = 256
FF_BLOCK = 1024
SCATTER_ROWS = 512


def _block_t(t, want):
    return want if t % want == 0 else t


def _pad_rows(w, top):
    out = jnp.zeros((LANES, w.shape[1]), w.dtype)
    return lax.dynamic_update_slice(out, w, (top, 0))


def kernel(x, c, ctx, c_ctx, ada_w, ada_b, w_in, shift_conv, w0, w2, a0, a2, g2, k_k, k_a, r_k, lnx_g, lnx_b,
           sgu_ln_g, sgu_ln_b, sgu_w, sgu_b, w_branch_a, w_branch_b, w_out, ln1_g, ln1_b, router_w, exp_w1,
           exp_w3, exp_w2, ln2_g, ln2_b):
    bsz, seq, d = x.shape
    depth = w_in.shape[0]
    n_rwkv = shift_conv.shape[-1]
    r_w, r_a, r_g = w2.shape[2], a2.shape[2], g2.shape[1]
    n_exp = router_w.shape[-1]
    assert 2 * r_w == LANES and 2 * r_a == LANES and r_g == LANES and n_rwkv == 3 * d + 3 * LANES
    assert r_k.shape[2] == HEAD and d % LANES == 0
    alpha = (2 * depth) ** 0.25

    n_rows = -(-(bsz + 1) // 8) * 8
    cc = jnp.zeros((n_rows, d), _F32).at[:bsz].set(c).at[bsz].set(c_ctx)
    mods = _ada_modulation(cc, ada_w, ada_b, block_n=d)

    s_zero = jnp.zeros((bsz, d // LANES, LANES, LANES), _F32)
    vec = lambda p: p.reshape(1, d)

    ew1, ew3, ew2 = exp_w1.astype(_BF16), exp_w3.astype(_BF16), exp_w2.astype(_BF16)

    for l in range(depth):
        mod_x = mods[l, :bsz].reshape(bsz, 1, 6, d)
        mod_c = mods[l, bsz].reshape(1, 1, 6, d)
        lat_mod = [mod_x[:, :, i] for i in range(6)]
        ctx_mod = [mod_c[:, :, i] for i in range(6)]
        w_rwkv = w_in[l, :, :n_rwkv].astype(_BF16)
        w_sgu_gate = w_in[l, :, n_rwkv:].astype(_BF16)
        prep_w = dict(conv=shift_conv[l], w0=w0[l],
                      w2f=_pad_rows(w2[l, 0], 0).astype(_BF16), w2b=_pad_rows(w2[l, 1], r_w).astype(_BF16),
                      a0=a0[l], a2f=_pad_rows(a2[l, 0], 0).astype(_BF16),
                      a2b=_pad_rows(a2[l, 1], r_a).astype(_BF16), g2=g2[l].astype(_BF16),
                      k_k=vec(k_k[l]), k_a=vec(k_a[l]))
        chunk = sgu_w.shape[-1]
        sgu_bias = jnp.repeat(sgu_b[l].T, d // sgu_w.shape[1], axis=1)
        mix_w = (vec(lnx_g[l]), vec(lnx_b[l]), r_k[l].reshape(1, d), w_branch_a[l].astype(_BF16),
                 w_branch_b[l].astype(_BF16), w_out[l].astype(_BF16), vec(ln1_g[l]), vec(ln1_b[l]))
        rw_pad = jnp.zeros((d, LANES), _F32).at[:, :n_exp].set(router_w[l])

        def mixer(tok, mod, s0_f, s0_b, full):
            t = tok.shape[1]
            bt = _block_t(t, TOK_BLOCK)
            z_r = _ln_mod_matmul(tok, mod[0], mod[1], w_rwkv, block_t=bt)
            r, v, kk, kf, kb, ldf, ldb, bef, beb, g = _rwkv_prep(z_r, d=d, block_t=_block_t(t, TOK_BLOCK_WIDE), **prep_w)
            y_f, s_f = _rwkv_scan(r, kf, v, ldf, kk, bef, s0_f, reverse=False, block_t=_block_t(t, TOK_BLOCK))
            y_b, s_b = _rwkv_scan(r, kb, v, ldb, kk, beb, s0_b, reverse=True, block_t=_block_t(t, TOK_BLOCK))
            if not full:
                return None, s_f, s_b
            z_sg = _ln_mod_matmul(tok, mod[0], mod[1], w_sgu_gate, block_t=bt)
            y_s = _spatial_gating(z_sg, vec(sgu_ln_g[l]), vec(sgu_ln_b[l]), sgu_w[l].astype(_BF16), sgu_bias,
                                  block_t=_block_t(t, 2 * chunk))
            out = _merge(y_f, y_b, r, kf, v, g, y_s, z_sg, tok, mod[2], *mix_w, alpha=alpha,
                         block_t=_block_t(t, TOK_BLOCK_WIDE))
            return out, s_f, s_b

        def moe(streams):
            sets = []
            for tok, mod in streams:
                h, aff = _router(tok, mod[3], mod[4], rw_pad, n_exp=n_exp, block_t=_block_t(tok.shape[1], TOK_BLOCK))
                aff_t = jnp.swapaxes(aff[:, :, :n_exp], 1, 2)
                slots = _expert_choice_slots(aff_t, cap=CAP_FACTOR * tok.shape[1] // n_exp)
                sets.append((h, slots[:, :, None, :], aff))
            ys = _expert_choice_ffn(sets, ew1, ew3, ew2, layer=l, block_f=_block_t(ew1.shape[-1], FF_BLOCK))
            return [_post_norm(tok, y, mod[5], vec(ln2_g[l]), vec(ln2_b[l]), alpha=alpha,
                               block_t=_block_t(tok.shape[1], TOK_BLOCK)) for (tok, mod), y in zip(streams, ys)]

        last = l == depth - 1
        ctx_mixed, s_f, s_b = mixer(ctx, ctx_mod, s_zero, s_zero, full=not last)
        x, _, _ = mixer(x, lat_mod, s_f, s_b, full=True)
        if last:
            x, = moe([(x, lat_mod)])
        else:
            x, ctx = moe([(x, lat_mod), (ctx_mixed, ctx_mod)])
    return x
```

```python
import functools
import math

import jax
import jax.numpy as jnp
from jax import lax
from jax.experimental import pallas as pl
from jax.experimental.pallas import tpu as pltpu

LANES = 128
SUBLANES = 8
HEAD = 64
SCAN_CHUNK = 64
INV_BLOCK = 16
CAP_FACTOR = 2
LN_EPS = 1e-5
GN_EPS = 64e-5
VMEM_LIMIT = 56 * 1024 * 1024

_F32 = jnp.float32
_BF16 = jnp.bfloat16


def _cparams(sem):
    return pltpu.CompilerParams(dimension_semantics=sem, vmem_limit_bytes=VMEM_LIMIT)


def _bdot(a, b):
    return jnp.dot(a.astype(_BF16), b.astype(_BF16), preferred_element_type=_F32)


def _stack_heads(x, lane_lo):
    return jnp.concatenate([jnp.where(lane_lo, x, 0.0), jnp.where(lane_lo, 0.0, x)], axis=0)


def _pdot(a, b):
    bb = b.astype(_BF16)
    b0, b1 = bb[:, :LANES], bb[:, LANES:]
    z = jnp.zeros_like(b0)
    rhs = jnp.concatenate([jnp.concatenate([b0, z], axis=1), jnp.concatenate([z, b1], axis=1)], axis=0)
    return jnp.dot(a.astype(_BF16), rhs, preferred_element_type=_F32)


def _unit_tri_inverses(mats):
    n = mats[0].shape[0]
    row = lax.broadcasted_iota(jnp.int32, (n, 2 * n), 0)
    col = lax.broadcasted_iota(jnp.int32, (n, 2 * n), 1) % n
    eye = (row == col).astype(_F32)
    same = row // INV_BLOCK == col // INV_BLOCK
    p = [jnp.where(same, -a, 0.0) for a in mats]
    t = [eye + x for x in p]
    for _ in range(3):
        p = [_pdot(x, x) for x in p]
        t = [ti + _pdot(ti, x) for ti, x in zip(t, p)]
    e = [_pdot(ti, jnp.where(same, 0.0, a)) for ti, a in zip(t, mats)]
    e2 = [_pdot(x, x) for x in e]
    f = [eye - x + x2 - _pdot(x, x2) for x, x2 in zip(e, e2)]
    return [_pdot(fi, ti) for fi, ti in zip(f, t)]


def _scan_kernel(r_ref, k_ref, v_ref, ld_ref, kk_ref, be_ref, s0_ref, *rest, reverse, n_chunks, add_to):
    yadd_ref = rest[0] if add_to else None
    y_ref, s_ref, qm_scr, y0_scr, n0_scr, dec_scr = rest[1:] if add_to else rest
    _scan_body(r_ref, k_ref, v_ref, ld_ref, kk_ref, be_ref, s0_ref, yadd_ref, y_ref, s_ref,
               qm_scr, y0_scr, n0_scr, dec_scr, reverse=reverse, n_chunks=n_chunks)


def _scan_body(r_ref, k_ref, v_ref, ld_ref, kk_ref, be_ref, s0_ref, yadd_ref, y_ref, s_ref,
               qm_scr, y0_scr, n0_scr, dec_scr, *, reverse, n_chunks):
    t_idx = pl.program_id(1)

    @pl.when(t_idx == 0)
    def _():
        s_ref[...] = s0_ref[...]

    c = SCAN_CHUNK
    pairs = range(r_ref.shape[-1] // LANES)
    lanes = [slice(p * LANES, (p + 1) * LANES) for p in pairs]
    ri = lax.broadcasted_iota(jnp.int32, (c, c), 0)
    ci = lax.broadcasted_iota(jnp.int32, (c, c), 1)
    tri = (ci >= ri if reverse else ci <= ri).astype(_F32)
    lane_lo = lax.broadcasted_iota(jnp.int32, (c, LANES), 1) < HEAD
    r2 = lax.broadcasted_iota(jnp.int32, (2 * c, 2 * c), 0)
    c2 = lax.broadcasted_iota(jnp.int32, (2 * c, 2 * c), 1)
    eye = (r2 == c2).astype(_F32)
    strict = (c2 % c > r2 % c) if reverse else (c2 % c < r2 % c)
    read_mask = strict if reverse else (c2 % c <= r2 % c)
    stack = lambda x: _stack_heads(x, lane_lo)
    contract_rows = (((0,), (0,)), ((), ()))

    group = 2 if n_chunks % 2 == 0 else 1
    pack = lambda xs: [jnp.concatenate([xs[i], xs[i + 1]], axis=1) for i in range(0, len(xs), 2)]
    unpack = lambda xs: [h for x in xs for h in (x[:, :LANES], x[:, LANES:])]

    def chunk_terms(g, carry):
        js = [g * group + dj for dj in range(group)]
        rows_of = [pl.ds(pl.multiple_of(j * c, c), c) for j in js]
        c_all = [jnp.dot(tri, ld_ref[0, rows, :], precision=lax.Precision.HIGHEST, preferred_element_type=_F32)
                 for rows in rows_of]
        units = [(dj, p) for dj in range(group) for p in pairs]
        load = lambda ref: [ref[0, rows_of[dj], lanes[p]].astype(_F32) for dj, p in units]
        c_in = [c_all[dj][:, lanes[p]] for dj, p in units]
        tot = [x[0:1, :] if reverse else x[c - 1:c, :] for x in c_in]
        e_ex = [jnp.exp(x - l) for x, l in zip(c_in, load(ld_ref))]
        e_rd = e_ex if reverse else [jnp.exp(x) for x in c_in]
        e_ninv = [jnp.exp(-x) for x in c_in]
        e_tot = [jnp.exp(t - x) for t, x in zip(tot, c_in)]
        kx = load(k_ref)
        be = load(be_ref)
        kap = [stack(x * e) for x, e in zip(load(kk_ref), e_ex)]
        rr = [stack(x * e) for x, e in zip(load(r_ref), e_rd)]
        lhs = [jnp.concatenate([a, b], axis=0).astype(_BF16) for a, b in zip(kap, rr)]
        rhs = [jnp.concatenate([stack(k * e), stack(b * e)], axis=0).astype(_BF16)
               for k, b, e in zip(kx, be, e_ninv)]
        aa = [lax.dot_general(a, b, (((1,), (1,)), ((), ())), preferred_element_type=_F32)
              for a, b in zip(lhs, rhs)]
        tf = unpack(_unit_tri_inverses(pack([jnp.where(strict, x[:2 * c, 2 * c:], 0.0) for x in aa])))
        vm = [stack(x) for x in load(v_ref)]
        a_v = [jnp.concatenate([jnp.where(strict, x[:2 * c, :2 * c], 0.0),
                                jnp.where(read_mask, x[2 * c:, :2 * c], 0.0)], axis=0) for x in aa]
        av = unpack([_pdot(a, v) for a, v in zip(pack(a_v), pack(vm))])
        pu = [_bdot(t, jnp.concatenate([a, b[:2 * c]], axis=1)).astype(_BF16) for t, a, b in zip(tf, kap, av)]
        arb = [_bdot(jnp.where(read_mask, x[2 * c:, 2 * c:], 0.0), z) for x, z in zip(aa, pu)]
        bhat = [stack(b * e).astype(_BF16) for b, e in zip(be, e_tot)]
        khat = [stack(k * e).astype(_BF16) for k, e in zip(kx, e_tot)]
        bp = [lax.dot_general(b, z, contract_rows, preferred_element_type=_F32) for b, z in zip(bhat, pu)]
        kv = [lax.dot_general(k, v.astype(_BF16), contract_rows, preferred_element_type=_F32)
              for k, v in zip(khat, vm)]
        for u, (dj, p) in enumerate(units):
            j = js[dj]
            q = rr[u] - arb[u][:, :2 * c]
            y0 = av[u][2 * c:] - arb[u][:, 2 * c:]
            half = slice((p % 2) * LANES, (p % 2 + 1) * LANES)
            qm_scr[j, p // 2, :c, half] = (q[:c] + q[c:]).astype(_BF16)
            qm_scr[j, p // 2, c:, half] = (-bp[u][:, :2 * c]).astype(_BF16)
            y0_scr[j, p // 2, :, half] = y0[:c] + y0[c:]
            n0_scr[j, p // 2, :, half] = kv[u] - bp[u][:, 2 * c:]
            dec = jnp.sum(jnp.where(eye > 0, jnp.broadcast_to(jnp.exp(tot[u]), (2 * c, LANES)), 0.0),
                          axis=1, keepdims=True)
            dec_scr[j, p // 2, :, half] = jnp.broadcast_to(dec, (2 * c, LANES))
        return carry

    lax.fori_loop(0, n_chunks // group, chunk_terms, 0)

    def advance(ic, carry):
        j = (n_chunks - 1 - ic) if reverse else ic
        rows = pl.ds(pl.multiple_of(j * c, c), c)
        s = pack([s_ref[0, p] for p in pairs])
        qm = [_pdot(qm_scr[j, i], s[i]) for i in range(len(s))]
        for i in range(len(s)):
            cols = slice(2 * i * LANES, 2 * (i + 1) * LANES)
            y = qm[i][:c] + y0_scr[j, i]
            y_ref[0, rows, cols] = y if yadd_ref is None else y + yadd_ref[0, rows, cols]
            s_new = dec_scr[j, i] * s[i] + qm[i][c:] + n0_scr[j, i]
            s_ref[0, 2 * i] = s_new[:, :LANES]
            s_ref[0, 2 * i + 1] = s_new[:, LANES:]
        return carry

    lax.fori_loop(0, n_chunks, advance, 0)


def _rwkv_scan(r, k, v, logd, kk, beta, s0, *, reverse, block_t, add_to=None):
    b, t, d = r.shape
    assert t % block_t == 0 and block_t % SCAN_CHUNK == 0 and d % (2 * LANES) == 0 and 2 * HEAD == LANES
    nt = t // block_t
    tmap = (lambda bi, ti: (bi, nt - 1 - ti, 0)) if reverse else (lambda bi, ti: (bi, ti, 0))
    tok = pl.BlockSpec((1, block_t, d), tmap)
    st = pl.BlockSpec((1, d // LANES, LANES, LANES), lambda bi, ti: (bi, 0, 0, 0))
    n_chunks, n_pairs = block_t // SCAN_CHUNK, d // LANES
    return pl.pallas_call(
        functools.partial(_scan_kernel, reverse=reverse, n_chunks=n_chunks, add_to=add_to is not None),
        grid=(b, nt),
        in_specs=[tok] * 6 + [st] + ([tok] if add_to is not None else []),
        out_specs=[tok, st],
        out_shape=[jax.ShapeDtypeStruct((b, t, d), _F32),
                   jax.ShapeDtypeStruct((b, d // LANES, LANES, LANES), _F32)],
        scratch_shapes=[pltpu.VMEM((n_chunks, n_pairs // 2, SCAN_CHUNK + LANES, 2 * LANES), _BF16),
                        pltpu.VMEM((n_chunks, n_pairs // 2, SCAN_CHUNK, 2 * LANES), _F32),
                        pltpu.VMEM((n_chunks, n_pairs // 2, LANES, 2 * LANES), _F32),
                        pltpu.VMEM((n_chunks, n_pairs // 2, LANES, 2 * LANES), _F32)],
        compiler_params=_cparams(("parallel", "arbitrary")),
        name="rwkv_scan_rev" if reverse else "rwkv_scan_fwd",
    )(r, k, v, logd, kk, beta, s0, *([add_to] if add_to is not None else []))


def _layer_norm(x, eps=LN_EPS):
    mu = jnp.mean(x, axis=-1, keepdims=True)
    xc = x - mu
    var = jnp.mean(xc * xc, axis=-1, keepdims=True)
    return xc * lax.rsqrt(var + eps)


def _sigmoid(x):
    return 0.5 * (1.0 + jnp.tanh(0.5 * x))


def _head_sum(x):
    row = lax.broadcasted_iota(jnp.int32, (LANES, LANES), 0) // HEAD
    col = lax.broadcasted_iota(jnp.int32, (LANES, LANES), 1) // HEAD
    ones = (row == col).astype(_BF16)
    hi = x.astype(_BF16)
    lo = (x - hi.astype(_F32)).astype(_BF16)
    slabs = []
    for s in range(0, x.shape[-1], LANES):
        slabs.append(jnp.dot(hi[:, s:s + LANES], ones, preferred_element_type=_F32)
                     + jnp.dot(lo[:, s:s + LANES], ones, preferred_element_type=_F32))
    return jnp.concatenate(slabs, axis=-1)


def _mod_map(n_mod):
    return (lambda bi, ti: (bi, 0, 0)) if n_mod > 1 else (lambda bi, ti: (0, 0, 0))


def _vec_spec(d):
    return pl.BlockSpec((1, d), lambda bi, ti: (0, 0))


def _full_spec(shape):
    return pl.BlockSpec(shape, lambda bi, ti: (0,) * len(shape))


def _tok_spec(tm, d):
    return pl.BlockSpec((1, tm, d), lambda bi, ti: (bi, ti, 0))


def _ada_kernel(c_ref, w_ref, b_ref, o_ref):
    c = c_ref[...]
    o_ref[0] = _bdot(c * _sigmoid(c), w_ref[0]) + b_ref[0]


def _ada_modulation(cc, ada_w, ada_b, *, block_n):
    depth, d, n = ada_w.shape
    rows = cc.shape[0]
    return pl.pallas_call(
        _ada_kernel,
        grid=(depth, n // block_n),
        in_specs=[pl.BlockSpec((rows, d), lambda li, ni: (0, 0)),
                  pl.BlockSpec((1, d, block_n), lambda li, ni: (li, 0, ni)),
                  pl.BlockSpec((1, 1, block_n), lambda li, ni: (li, 0, ni))],
        out_specs=pl.BlockSpec((1, rows, block_n), lambda li, ni: (li, 0, ni)),
        out_shape=jax.ShapeDtypeStruct((depth, rows, n), _F32),
        compiler_params=_cparams(("parallel", "parallel")),
        name="ada_modulation",
    )(cc, ada_w, ada_b.reshape(depth, 1, n))


def _ln_mod_matmul_kernel(x_ref, sh_ref, sc_ref, w_ref, o_ref):
    h = _layer_norm(x_ref[0]) * (1.0 + sc_ref[0]) + sh_ref[0]
    o_ref[0] = _bdot(h, w_ref[...])


def _ln_mod_matmul(x, shift, scale, w, *, block_t):
    b, t, d = x.shape
    n = w.shape[1]
    mod = pl.BlockSpec((1, 1, d), _mod_map(shift.shape[0]))
    return pl.pallas_call(
        _ln_mod_matmul_kernel,
        grid=(b, t // block_t),
        in_specs=[_tok_spec(block_t, d), mod, mod, _full_spec((d, n))],
        out_specs=_tok_spec(block_t, n),
        out_shape=jax.ShapeDtypeStruct((b, t, n), _F32),
        compiler_params=_cparams(("parallel", "parallel")),
        name="ln_mod_matmul",
    )(x, shift, scale, w)


def _prep_kernel(z_ref, zp_ref, zn_ref, conv_ref, w0_ref, w2f_ref, w2b_ref, a0_ref, a2f_ref, a2b_ref,
                 g2_ref, kk_ref, ka_ref,
                 r_o, v_o, kkn_o, kf_o, kb_o, ldf_o, ldb_o, bf_o, bb_o, g_o):
    z = z_ref[0]
    tm = z.shape[0]
    d = r_o.shape[-1]
    row = lax.broadcasted_iota(jnp.int32, z.shape, 0)
    t_idx = pl.program_id(1)
    before = jnp.where(t_idx == 0, 0.0, zp_ref[0, SUBLANES - 1:SUBLANES, :])
    after = jnp.where(t_idx == pl.num_programs(1) - 1, 0.0, zn_ref[0, 0:1, :])
    up = jnp.where(row == 0, before, pltpu.roll(z, 1, 0))
    dn = jnp.where(row == tm - 1, after, pltpu.roll(z, tm - 1, 0))
    conv = conv_ref[...]
    zs = up * conv[0:1] + z * conv[1:2] + dn * conv[2:3]
    r = zs[:, 0:d]
    k = zs[:, d:2 * d]
    v = zs[:, 2 * d:3 * d]
    tw = jnp.tanh(zs[:, 3 * d:3 * d + LANES])
    la = zs[:, 3 * d + LANES:3 * d + 2 * LANES]
    dg = zs[:, 3 * d + 2 * LANES:3 * d + 3 * LANES]

    def log_decay(w0, w2_ref):
        return -math.exp(-0.5) * _sigmoid(w0 + _bdot(tw, w2_ref[...]))

    w0 = w0_ref[...]
    a0 = a0_ref[...]
    a_f = _sigmoid(a0[0:1] + _bdot(la, a2f_ref[...]))
    a_b = _sigmoid(a0[1:2] + _bdot(la, a2b_ref[...]))
    kkr = k * kk_ref[...]
    kkn = kkr * lax.rsqrt(jnp.maximum(_head_sum(kkr * kkr), 1e-12))
    ka = ka_ref[...]
    r_o[0] = r.astype(_BF16)
    v_o[0] = v.astype(_BF16)
    kkn_o[0] = kkn.astype(_BF16)
    kf_o[0] = (k * (1.0 + (a_f - 1.0) * ka)).astype(_BF16)
    kb_o[0] = (k * (1.0 + (a_b - 1.0) * ka)).astype(_BF16)
    ldf_o[0] = log_decay(w0[0:1], w2f_ref)
    ldb_o[0] = log_decay(w0[1:2], w2b_ref)
    bf_o[0] = (kkn * a_f).astype(_BF16)
    bb_o[0] = (kkn * a_b).astype(_BF16)
    g_o[0] = _bdot(_sigmoid(dg), g2_ref[...]).astype(_BF16)


def _rwkv_prep(z, conv, w0, w2f, w2b, a0, a2f, a2b, g2, k_k, k_a, *, d, block_t):
    b, t, nr = z.shape
    nb = t // block_t
    tiles = block_t // SUBLANES
    prev_tile = pl.BlockSpec((1, SUBLANES, nr), lambda bi, ti: (bi, jnp.maximum(ti * tiles - 1, 0), 0))
    next_tile = pl.BlockSpec((1, SUBLANES, nr), lambda bi, ti: (bi, jnp.minimum((ti + 1) * tiles, nb * tiles - 1), 0))
    f32, b16 = jax.ShapeDtypeStruct((b, t, d), _F32), jax.ShapeDtypeStruct((b, t, d), _BF16)
    return pl.pallas_call(
        _prep_kernel,
        grid=(b, nb),
        in_specs=[_tok_spec(block_t, nr), prev_tile, next_tile, _full_spec(conv.shape), _full_spec(w0.shape),
                  _full_spec(w2f.shape), _full_spec(w2b.shape), _full_spec(a0.shape), _full_spec(a2f.shape),
                  _full_spec(a2b.shape), _full_spec(g2.shape), _vec_spec(d), _vec_spec(d)],
        out_specs=[_tok_spec(block_t, d)] * 10,
        out_shape=[b16, b16, b16, b16, b16, f32, f32, b16, b16, b16],
        compiler_params=_cparams(("parallel", "parallel")),
        name="rwkv_prep",
    )(z, z, z, conv, w0, w2f, w2b, a0, a2f, a2b, g2, k_k, k_a)


def _sgu_kernel(z_ref, lg_ref, lb_ref, w_ref, bias_ref, o_ref, *, chunk):
    z = z_ref[0]
    ge = 0.5 * z * (1.0 + lax.erf(z * (2.0 ** -0.5)))
    d = o_ref.shape[-1]
    u = ge[:, :d]
    v = _layer_norm(ge[:, d:]) * lg_ref[...] + lb_ref[...]
    bias = bias_ref[...]
    for c in range(z.shape[0] // chunk):
        rows = slice(c * chunk, (c + 1) * chunk)
        for g in range(w_ref.shape[0]):
            cols = slice(g * (d // w_ref.shape[0]), (g + 1) * (d // w_ref.shape[0]))
            mixed = _bdot(w_ref[g], v[rows, cols]) + bias[:, cols]
            o_ref[0, rows, cols] = (u[rows, cols] * mixed).astype(o_ref.dtype)


def _spatial_gating(z_sgu, ln_g, ln_b, w, bias_full, *, block_t):
    b, t, _ = z_sgu.shape
    d = ln_g.shape[-1]
    d2 = 2 * d
    chunk = w.shape[-1]
    return pl.pallas_call(
        functools.partial(_sgu_kernel, chunk=chunk),
        grid=(b, t // block_t),
        in_specs=[_tok_spec(block_t, d2), _vec_spec(d), _vec_spec(d), _full_spec(w.shape),
                  _full_spec(bias_full.shape)],
        out_specs=_tok_spec(block_t, d),
        out_shape=jax.ShapeDtypeStruct((b, t, d), _BF16),
        compiler_params=_cparams(("parallel", "parallel")),
        name="spatial_gating",
    )(z_sgu, ln_g, ln_b, w, bias_full)


def _merge_kernel(y_ref, r_ref, kf_ref, v_ref, g_ref, ys_ref, zg_ref, x_ref, gt_ref,
                  lnxg_ref, lnxb_ref, rk_ref, wa_ref, wb_ref, wo_ref, l1g_ref, l1b_ref, o_ref, *, alpha):
    d = o_ref.shape[-1]
    y = y_ref[0]
    yc = y - _head_sum(y) * (1.0 / HEAD)
    var = _head_sum(yc * yc) * (1.0 / HEAD)
    yn = yc * lax.rsqrt(var + GN_EPS) * lnxg_ref[...] + lnxb_ref[...]
    f32 = lambda ref: ref[0].astype(_F32)
    bonus = _head_sum(f32(r_ref) * f32(kf_ref) * rk_ref[...]) * f32(v_ref)
    y_a = (yn + bonus) * f32(g_ref)
    zg = zg_ref[0]
    merged = _sigmoid(zg[:, :d]) * _bdot(y_a, wa_ref[...]) + _sigmoid(zg[:, d:]) * _bdot(ys_ref[0], wb_ref[...])
    out = _bdot(merged, wo_ref[...])
    o_ref[0] = _layer_norm(alpha * x_ref[0] + gt_ref[0] * out) * l1g_ref[...] + l1b_ref[...]


def _merge(y, r, kf, v, g, ys, z_sg, x, gate, lnx_g, lnx_b, r_k, wa, wb, wo, l1g, l1b, *, alpha, block_t):
    b, t, d = x.shape
    tok = _tok_spec(block_t, d)
    vec = _vec_spec(d)
    mat = _full_spec((d, d))
    gate_half = pl.BlockSpec((1, block_t, 2 * d), lambda bi, ti: (bi, ti, 1))
    return pl.pallas_call(
        functools.partial(_merge_kernel, alpha=alpha),
        grid=(b, t // block_t),
        in_specs=[tok] * 6 + [gate_half, tok, pl.BlockSpec((1, 1, d), _mod_map(gate.shape[0])),
                  vec, vec, vec, mat, mat, mat, vec, vec],
        out_specs=tok,
        out_shape=jax.ShapeDtypeStruct((b, t, d), _F32),
        compiler_params=_cparams(("parallel", "parallel")),
        name="branch_merge",
    )(y, r, kf, v, g, ys, z_sg, x, gate, lnx_g, lnx_b, r_k, wa, wb, wo, l1g, l1b)


def _split_bf16(x):
    hi = x.astype(_BF16)
    return hi, (x - hi.astype(_F32)).astype(_BF16)


def _router_kernel(x_ref, sh_ref, sc_ref, rw_ref, h_ref, aff_ref, *, n_exp):
    h = _layer_norm(x_ref[0]) * (1.0 + sc_ref[0]) + sh_ref[0]
    h_ref[0] = h.astype(_BF16)
    h_hi, h_lo = _split_bf16(h)
    w_hi, w_lo = _split_bf16(rw_ref[...])
    dot = lambda a, b: jnp.dot(a, b, preferred_element_type=_F32)
    logits = dot(h_hi, w_hi) + (dot(h_hi, w_lo) + dot(h_lo, w_hi))
    lane = lax.broadcasted_iota(jnp.int32, logits.shape, 1)
    logits = jnp.where(lane < n_exp, logits, -jnp.inf)
    e = jnp.exp(logits - jnp.max(logits, axis=-1, keepdims=True))
    aff_ref[0] = e / jnp.sum(e, axis=-1, keepdims=True)


def _router(x, shift, scale, rw_pad, *, n_exp, block_t):
    b, t, d = x.shape
    mod = pl.BlockSpec((1, 1, d), _mod_map(shift.shape[0]))
    return pl.pallas_call(
        functools.partial(_router_kernel, n_exp=n_exp),
        grid=(b, t // block_t),
        in_specs=[_tok_spec(block_t, d), mod, mod, _full_spec(rw_pad.shape)],
        out_specs=[_tok_spec(block_t, d), _tok_spec(block_t, LANES)],
        out_shape=[jax.ShapeDtypeStruct((b, t, d), _BF16), jax.ShapeDtypeStruct((b, t, LANES), _F32)],
        compiler_params=_cparams(("parallel", "parallel")),
        name="moe_router",
    )(x, shift, scale, rw_pad)


def _count_before(mask):
    t = mask.shape[1]
    tile = min(t, PREFIX_TILE)
    earlier = (lax.broadcasted_iota(jnp.int32, (tile, tile), 0)
               < lax.broadcasted_iota(jnp.int32, (tile, tile), 1)).astype(_BF16)
    pieces, total = [], jnp.zeros((mask.shape[0], 1), _F32)
    for t0 in range(0, t, tile):
        m = mask[:, t0:t0 + tile]
        pieces.append(jnp.dot(m.astype(_BF16), earlier, preferred_element_type=_F32) + total)
        total = total + jnp.sum(m, axis=1, keepdims=True)
    return jnp.concatenate(pieces, axis=1)


def _select_kernel(aff_ref, slot_ref, *, cap):
    a = aff_ref[0]
    key = lax.bitcast_convert_type(a, jnp.int32)
    thr = jnp.zeros((a.shape[0], 1), jnp.int32)
    for bit in range(30, -1, -1):
        cand = thr | (1 << bit)
        count = jnp.sum((key >= cand).astype(_F32), axis=1, keepdims=True)
        thr = jnp.where(count >= cap, cand, thr)
    above = key > thr
    tied = key == thr
    need = cap - jnp.sum(above.astype(_F32), axis=1, keepdims=True)
    chosen = jnp.logical_or(above, jnp.logical_and(tied, _count_before(tied.astype(_F32)) < need))
    slot_ref[0] = jnp.where(chosen, _count_before(chosen.astype(_F32)), -1.0)


def _expert_choice_slots(aff_t, *, cap):
    b, n_exp, t = aff_t.shape
    return pl.pallas_call(
        functools.partial(_select_kernel, cap=cap),
        grid=(b,),
        in_specs=[pl.BlockSpec((1, n_exp, t), lambda bi: (bi, 0, 0))],
        out_specs=pl.BlockSpec((1, n_exp, t), lambda bi: (bi, 0, 0)),
        out_shape=jax.ShapeDtypeStruct((b, n_exp, t), _F32),
        compiler_params=_cparams(("parallel",)),
        name="expert_choice_slots",
    )(aff_t)


def _moe_kernel(*refs, n_sets):
    ins, rest = refs[:3 * n_sets], refs[3 * n_sets:]
    w1_ref, w3_ref, w2_ref = rest[:3]
    outs = rest[3:3 + n_sets]
    xe_ref, ye_ref = rest[3 + n_sets:5 + n_sets]
    p_refs = rest[5 + n_sets:]
    e_idx = pl.program_id(1)
    f_idx = pl.program_id(2)
    caps = [p.shape[0] for p in p_refs]
    offs = [sum(caps[:i]) for i in range(n_sets)]

    @pl.when(jnp.logical_and(e_idx == 0, f_idx == 0))
    def _():
        for o_ref in outs:
            o_ref[...] = jnp.zeros_like(o_ref)

    @pl.when(f_idx == 0)
    def _():
        for i in range(n_sets):
            h_ref, slot_ref = ins[3 * i:3 * i + 2]
            cap, t = p_refs[i].shape
            slot = lax.broadcasted_iota(jnp.int32, (cap, t), 0).astype(_F32)
            p = (slot_ref[0, 0] == slot).astype(_BF16)
            p_refs[i][...] = p
            xe_ref[offs[i]:offs[i] + cap, :] = jnp.dot(p, h_ref[0], preferred_element_type=_F32).astype(_BF16)
        ye_ref[...] = jnp.zeros_like(ye_ref)

    xe = xe_ref[...]
    h1 = jnp.dot(xe, w1_ref[0, 0], preferred_element_type=_F32)
    h3 = jnp.dot(xe, w3_ref[0, 0], preferred_element_type=_F32)
    ye_ref[...] += _bdot(h1 * _sigmoid(h1) * h3, w2_ref[0, 0])

    @pl.when(f_idx == pl.num_programs(2) - 1)
    def _():
        for i in range(n_sets):
            aff_ref, o_ref = ins[3 * i + 2], outs[i]
            cap, t = p_refs[i].shape
            blk = min(t, SCATTER_ROWS)
            ye = ye_ref[offs[i]:offs[i] + cap, :].astype(_BF16)
            this_expert = lax.broadcasted_iota(jnp.int32, (blk, LANES), 1) == e_idx
            for t0 in range(0, t, blk):
                back = lax.dot_general(p_refs[i][:, t0:t0 + blk], ye, (((0,), (0,)), ((), ())),
                                       preferred_element_type=_F32)
                gate = jnp.sum(jnp.where(this_expert, aff_ref[0, t0:t0 + blk, :], 0.0), axis=1, keepdims=True)
                o_ref[0, t0:t0 + blk, :] += gate * back


def _expert_choice_ffn(sets, w1, w3, w2, *, layer, block_f):
    _, n_exp, d, d_ff = w1.shape
    b = sets[0][0].shape[0]
    caps = [CAP_FACTOR * h.shape[1] // n_exp for h, _, _ in sets]
    in_specs, out_specs, out_shape, operands = [], [], [], []
    for h, slot, aff in sets:
        t = h.shape[1]
        in_specs += [pl.BlockSpec((1, t, d), lambda bi, ei, fi: (bi, 0, 0)),
                     pl.BlockSpec((1, 1, 1, t), lambda bi, ei, fi: (bi, ei, 0, 0)),
                     pl.BlockSpec((1, t, LANES), lambda bi, ei, fi: (bi, 0, 0))]
        out_specs.append(pl.BlockSpec((1, t, d), lambda bi, ei, fi: (bi, 0, 0)))
        out_shape.append(jax.ShapeDtypeStruct((b, t, d), _F32))
        operands += [h, slot, aff]
    in_specs += [pl.BlockSpec((1, 1, d, block_f), lambda bi, ei, fi: (layer, ei, 0, fi)),
                 pl.BlockSpec((1, 1, d, block_f), lambda bi, ei, fi: (layer, ei, 0, fi)),
                 pl.BlockSpec((1, 1, block_f, d), lambda bi, ei, fi: (layer, ei, fi, 0))]
    return pl.pallas_call(
        functools.partial(_moe_kernel, n_sets=len(sets)),
        grid=(b, n_exp, d_ff // block_f),
        in_specs=in_specs,
        out_specs=out_specs,
        out_shape=out_shape,
        scratch_shapes=[pltpu.VMEM((sum(caps), d), _BF16), pltpu.VMEM((sum(caps), d), _F32)]
        + [pltpu.VMEM((cap, h.shape[1]), _BF16) for cap, (h, _, _) in zip(caps, sets)],
        compiler_params=_cparams(("parallel", "arbitrary", "arbitrary")),
        name="expert_choice_ffn",
    )(*operands, w1, w3, w2)


def _post_norm_kernel(x_ref, y_ref, gt_ref, g_ref, b_ref, o_ref, *, alpha):
    o_ref[0] = _layer_norm(alpha * x_ref[0] + gt_ref[0] * y_ref[0]) * g_ref[...] + b_ref[...]


def _post_norm(x, y, gate, g, bias, *, alpha, block_t):
    b, t, d = x.shape
    tok = _tok_spec(block_t, d)
    return pl.pallas_call(
        functools.partial(_post_norm_kernel, alpha=alpha),
        grid=(b, t // block_t),
        in_specs=[tok, tok, pl.BlockSpec((1, 1, d), _mod_map(gate.shape[0])), _vec_spec(d), _vec_spec(d)],
        out_specs=tok,
        out_shape=jax.ShapeDtypeStruct((b, t, d), _F32),
        compiler_params=_cparams(("parallel", "parallel")),
        name="post_norm",
    )(x, y, gate, g, bias)


TOK_BLOCK = 512
TOK_BLOCK_WIDE = 256
FF_BLOCK = 1024
SCATTER_ROWS = 512
PREFIX_TILE = 256


def _block_t(t, want):
    return want if t % want == 0 else t


def _pad_rows(w, top):
    out = jnp.zeros((LANES, w.shape[1]), w.dtype)
    return lax.dynamic_update_slice(out, w, (top, 0))


def kernel(x, c, ctx, c_ctx, ada_w, ada_b, w_in, shift_conv, w0, w2, a0, a2, g2, k_k, k_a, r_k, lnx_g, lnx_b,
           sgu_ln_g, sgu_ln_b, sgu_w, sgu_b, w_branch_a, w_branch_b, w_out, ln1_g, ln1_b, router_w, exp_w1,
           exp_w3, exp_w2, ln2_g, ln2_b):
    bsz, seq, d = x.shape
    depth = w_in.shape[0]
    n_rwkv = shift_conv.shape[-1]
    r_w, r_a, r_g = w2.shape[2], a2.shape[2], g2.shape[1]
    n_exp = router_w.shape[-1]
    assert 2 * r_w == LANES and 2 * r_a == LANES and r_g == LANES and n_rwkv == 3 * d + 3 * LANES
    assert r_k.shape[2] == HEAD and d % LANES == 0
    alpha = (2 * depth) ** 0.25

    n_rows = -(-(bsz + 1) // 8) * 8
    cc = jnp.zeros((n_rows, d), _F32).at[:bsz].set(c).at[bsz].set(c_ctx)
    mods = _ada_modulation(cc, ada_w, ada_b, block_n=d)

    s_zero = jnp.zeros((bsz, d // LANES, LANES, LANES), _F32)
    vec = lambda p: p.reshape(1, d)

    ew1, ew3, ew2 = exp_w1.astype(_BF16), exp_w3.astype(_BF16), exp_w2.astype(_BF16)

    for l in range(depth):
        mod_x = mods[l, :bsz].reshape(bsz, 1, 6, d)
        mod_c = mods[l, bsz].reshape(1, 1, 6, d)
        lat_mod = [mod_x[:, :, i] for i in range(6)]
        ctx_mod = [mod_c[:, :, i] for i in range(6)]
        w_rwkv = w_in[l, :, :n_rwkv].astype(_BF16)
        w_sgu_gate = w_in[l, :, n_rwkv:].astype(_BF16)
        prep_w = dict(conv=shift_conv[l], w0=w0[l],
                      w2f=_pad_rows(w2[l, 0], 0).astype(_BF16), w2b=_pad_rows(w2[l, 1], r_w).astype(_BF16),
                      a0=a0[l], a2f=_pad_rows(a2[l, 0], 0).astype(_BF16),
                      a2b=_pad_rows(a2[l, 1], r_a).astype(_BF16), g2=g2[l].astype(_BF16),
                      k_k=vec(k_k[l]), k_a=vec(k_a[l]))
        chunk = sgu_w.shape[-1]
        sgu_bias = jnp.repeat(sgu_b[l].T, d // sgu_w.shape[1], axis=1)
        mix_w = (vec(lnx_g[l]), vec(lnx_b[l]), r_k[l].reshape(1, d), w_branch_a[l].astype(_BF16),
                 w_branch_b[l].astype(_BF16), w_out[l].astype(_BF16), vec(ln1_g[l]), vec(ln1_b[l]))
        rw_pad = jnp.zeros((d, LANES), _F32).at[:, :n_exp].set(router_w[l])

        def mixer(tok, mod, s0_f, s0_b, full):
            t = tok.shape[1]
            bt = _block_t(t, TOK_BLOCK)
            z_r = _ln_mod_matmul(tok, mod[0], mod[1], w_rwkv, block_t=bt)
            r, v, kk, kf, kb, ldf, ldb, bef, beb, g = _rwkv_prep(z_r, d=d, block_t=_block_t(t, TOK_BLOCK_WIDE), **prep_w)
            y_f, s_f = _rwkv_scan(r, kf, v, ldf, kk, bef, s0_f, reverse=False, block_t=_block_t(t, TOK_BLOCK))
            y, s_b = _rwkv_scan(r, kb, v, ldb, kk, beb, s0_b, reverse=True, block_t=_block_t(t, TOK_BLOCK),
                                add_to=y_f)
            if not full:
                return None, s_f, s_b
            z_sg = _ln_mod_matmul(tok, mod[0], mod[1], w_sgu_gate, block_t=bt)
            y_s = _spatial_gating(z_sg, vec(sgu_ln_g[l]), vec(sgu_ln_b[l]), sgu_w[l].astype(_BF16), sgu_bias,
                                  block_t=_block_t(t, 2 * chunk))
            out = _merge(y, r, kf, v, g, y_s, z_sg, tok, mod[2], *mix_w, alpha=alpha,
                         block_t=_block_t(t, TOK_BLOCK_WIDE))
            return out, s_f, s_b

        def moe(streams):
            sets = []
            for tok, mod in streams:
                h, aff = _router(tok, mod[3], mod[4], rw_pad, n_exp=n_exp, block_t=_block_t(tok.shape[1], TOK_BLOCK))
                aff_t = jnp.swapaxes(aff[:, :, :n_exp], 1, 2)
                slots = _expert_choice_slots(aff_t, cap=CAP_FACTOR * tok.shape[1] // n_exp)
                sets.append((h, slots[:, :, None, :], aff))
            ys = _expert_choice_ffn(sets, ew1, ew3, ew2, layer=l, block_f=_block_t(ew1.shape[-1], FF_BLOCK))
            return [_post_norm(tok, y, mod[5], vec(ln2_g[l]), vec(ln2_b[l]), alpha=alpha,
                               block_t=_block_t(tok.shape[1], TOK_BLOCK)) for (tok, mod), y in zip(streams, ys)]

        last = l == depth - 1
        ctx_mixed, s_f, s_b = mixer(ctx, ctx_mod, s_zero, s_zero, full=not last)
        x, _, _ = mixer(x, lat_mod, s_f, s_b, full=True)
        if last:
            x, = moe([(x, lat_mod)])
        else:
            x, ctx = moe([(x, lat_mod), (ctx_mixed, ctx_mod)])
    return x
```

```python
import functools
import math

import jax
import jax.numpy as jnp
from jax import lax
from jax.experimental import pallas as pl
from jax.experimental.pallas import tpu as pltpu

LANES = 128
SUBLANES = 8
HEAD = 64
SCAN_CHUNK = 64
INV_BLOCK = 16
CAP_FACTOR = 2
LN_EPS = 1e-5
GN_EPS = 64e-5
VMEM_LIMIT = 56 * 1024 * 1024

_F32 = jnp.float32
_BF16 = jnp.bfloat16


def _cparams(sem):
    return pltpu.CompilerParams(dimension_semantics=sem, vmem_limit_bytes=VMEM_LIMIT)


def _bdot(a, b):
    return jnp.dot(a.astype(_BF16), b.astype(_BF16), preferred_element_type=_F32)


def _stack_heads(x, lane_lo):
    return jnp.concatenate([jnp.where(lane_lo, x, 0.0), jnp.where(lane_lo, 0.0, x)], axis=0)


def _pdot(a, b):
    bb = b.astype(_BF16)
    b0, b1 = bb[:, :LANES], bb[:, LANES:]
    z = jnp.zeros_like(b0)
    rhs = jnp.concatenate([jnp.concatenate([b0, z], axis=1), jnp.concatenate([z, b1], axis=1)], axis=0)
    return jnp.dot(a.astype(_BF16), rhs, preferred_element_type=_F32)


def _unit_tri_inverses(mats):
    n = mats[0].shape[0]
    row = lax.broadcasted_iota(jnp.int32, (n, 2 * n), 0)
    col = lax.broadcasted_iota(jnp.int32, (n, 2 * n), 1) % n
    eye = (row == col).astype(_F32)
    same = row // INV_BLOCK == col // INV_BLOCK
    p = [jnp.where(same, -a, 0.0) for a in mats]
    t = [eye + x for x in p]
    for _ in range(3):
        p = [_pdot(x, x) for x in p]
        t = [ti + _pdot(ti, x) for ti, x in zip(t, p)]
    e = [_pdot(ti, jnp.where(same, 0.0, a)) for ti, a in zip(t, mats)]
    e2 = [_pdot(x, x) for x in e]
    f = [eye - x + x2 - _pdot(x, x2) for x, x2 in zip(e, e2)]
    return [_pdot(fi, ti) for fi, ti in zip(f, t)]


def _scan_kernel(r_ref, k_ref, v_ref, ld_ref, kk_ref, be_ref, s0_ref, *rest, reverse, n_chunks, add_to):
    yadd_ref = rest[0] if add_to else None
    y_ref, s_ref, qm_scr, y0_scr, n0_scr, dec_scr = rest[1:] if add_to else rest
    _scan_body(r_ref, k_ref, v_ref, ld_ref, kk_ref, be_ref, s0_ref, yadd_ref, y_ref, s_ref,
               qm_scr, y0_scr, n0_scr, dec_scr, reverse=reverse, n_chunks=n_chunks)


def _scan_body(r_ref, k_ref, v_ref, ld_ref, kk_ref, be_ref, s0_ref, yadd_ref, y_ref, s_ref,
               qm_scr, y0_scr, n0_scr, dec_scr, *, reverse, n_chunks):
    t_idx = pl.program_id(1)

    @pl.when(t_idx == 0)
    def _():
        s_ref[...] = s0_ref[...]

    c = SCAN_CHUNK
    pairs = range(r_ref.shape[-1] // LANES)
    lanes = [slice(p * LANES, (p + 1) * LANES) for p in pairs]
    ri = lax.broadcasted_iota(jnp.int32, (c, c), 0)
    ci = lax.broadcasted_iota(jnp.int32, (c, c), 1)
    tri = (ci >= ri if reverse else ci <= ri).astype(_F32)
    lane_lo = lax.broadcasted_iota(jnp.int32, (c, LANES), 1) < HEAD
    r2 = lax.broadcasted_iota(jnp.int32, (2 * c, 2 * c), 0)
    c2 = lax.broadcasted_iota(jnp.int32, (2 * c, 2 * c), 1)
    eye = (r2 == c2).astype(_F32)
    strict = (c2 % c > r2 % c) if reverse else (c2 % c < r2 % c)
    read_mask = strict if reverse else (c2 % c <= r2 % c)
    stack = lambda x: _stack_heads(x, lane_lo)
    contract_rows = (((0,), (0,)), ((), ()))

    group = 2 if n_chunks % 2 == 0 else 1
    pack = lambda xs: [jnp.concatenate([xs[i], xs[i + 1]], axis=1) for i in range(0, len(xs), 2)]
    unpack = lambda xs: [h for x in xs for h in (x[:, :LANES], x[:, LANES:])]

    def chunk_terms(g, carry):
        js = [g * group + dj for dj in range(group)]
        rows_of = [pl.ds(pl.multiple_of(j * c, c), c) for j in js]
        c_all = [jnp.dot(tri, ld_ref[0, rows, :], precision=lax.Precision.HIGHEST, preferred_element_type=_F32)
                 for rows in rows_of]
        units = [(dj, p) for dj in range(group) for p in pairs]
        load = lambda ref: [ref[0, rows_of[dj], lanes[p]].astype(_F32) for dj, p in units]
        c_in = [c_all[dj][:, lanes[p]] for dj, p in units]
        tot = [x[0:1, :] if reverse else x[c - 1:c, :] for x in c_in]
        e_ex = [jnp.exp(x - l) for x, l in zip(c_in, load(ld_ref))]
        e_rd = e_ex if reverse else [jnp.exp(x) for x in c_in]
        e_ninv = [jnp.exp(-x) for x in c_in]
        e_tot = [jnp.exp(t - x) for t, x in zip(tot, c_in)]
        kx = load(k_ref)
        be = load(be_ref)
        kap = [stack(x * e) for x, e in zip(load(kk_ref), e_ex)]
        rr = [stack(x * e) for x, e in zip(load(r_ref), e_rd)]
        lhs = [jnp.concatenate([a, b], axis=0).astype(_BF16) for a, b in zip(kap, rr)]
        rhs = [jnp.concatenate([stack(k * e), stack(b * e)], axis=0).astype(_BF16)
               for k, b, e in zip(kx, be, e_ninv)]
        aa = [lax.dot_general(a, b, (((1,), (1,)), ((), ())), preferred_element_type=_F32)
              for a, b in zip(lhs, rhs)]
        tf = unpack(_unit_tri_inverses(pack([jnp.where(strict, x[:2 * c, 2 * c:], 0.0) for x in aa])))
        vm = [stack(x) for x in load(v_ref)]
        a_v = [jnp.concatenate([jnp.where(strict, x[:2 * c, :2 * c], 0.0),
                                jnp.where(read_mask, x[2 * c:, :2 * c], 0.0)], axis=0) for x in aa]
        av = unpack([_pdot(a, v) for a, v in zip(pack(a_v), pack(vm))])
        pu = [_bdot(t, jnp.concatenate([a, b[:2 * c]], axis=1)).astype(_BF16) for t, a, b in zip(tf, kap, av)]
        arb = [_bdot(jnp.where(read_mask, x[2 * c:, 2 * c:], 0.0), z) for x, z in zip(aa, pu)]
        bhat = [stack(b * e).astype(_BF16) for b, e in zip(be, e_tot)]
        khat = [stack(k * e).astype(_BF16) for k, e in zip(kx, e_tot)]
        bp = [lax.dot_general(b, z, contract_rows, preferred_element_type=_F32) for b, z in zip(bhat, pu)]
        kv = [lax.dot_general(k, v.astype(_BF16), contract_rows, preferred_element_type=_F32)
              for k, v in zip(khat, vm)]
        for u, (dj, p) in enumerate(units):
            j = js[dj]
            q = rr[u] - arb[u][:, :2 * c]
            y0 = av[u][2 * c:] - arb[u][:, 2 * c:]
            half = slice((p % 2) * LANES, (p % 2 + 1) * LANES)
            qm_scr[j, p // 2, :c, half] = (q[:c] + q[c:]).astype(_BF16)
            qm_scr[j, p // 2, c:, half] = (-bp[u][:, :2 * c]).astype(_BF16)
            y0_scr[j, p // 2, :, half] = y0[:c] + y0[c:]
            n0_scr[j, p // 2, :, half] = kv[u] - bp[u][:, 2 * c:]
            dec = jnp.sum(jnp.where(eye > 0, jnp.broadcast_to(jnp.exp(tot[u]), (2 * c, LANES)), 0.0),
                          axis=1, keepdims=True)
            dec_scr[j, p // 2, :, half] = jnp.broadcast_to(dec, (2 * c, LANES))
        return carry

    lax.fori_loop(0, n_chunks // group, chunk_terms, 0)

    def advance(ic, carry):
        j = (n_chunks - 1 - ic) if reverse else ic
        rows = pl.ds(pl.multiple_of(j * c, c), c)
        s = pack([s_ref[0, p] for p in pairs])
        qm = [_pdot(qm_scr[j, i], s[i]) for i in range(len(s))]
        for i in range(len(s)):
            cols = slice(2 * i * LANES, 2 * (i + 1) * LANES)
            y = qm[i][:c] + y0_scr[j, i]
            y_ref[0, rows, cols] = y if yadd_ref is None else y + yadd_ref[0, rows, cols]
            s_new = dec_scr[j, i] * s[i] + qm[i][c:] + n0_scr[j, i]
            s_ref[0, 2 * i] = s_new[:, :LANES]
            s_ref[0, 2 * i + 1] = s_new[:, LANES:]
        return carry

    lax.fori_loop(0, n_chunks, advance, 0)


def _rwkv_scan(r, k, v, logd, kk, beta, s0, *, reverse, block_t, add_to=None):
    b, t, d = r.shape
    assert t % block_t == 0 and block_t % SCAN_CHUNK == 0 and d % (2 * LANES) == 0 and 2 * HEAD == LANES
    nt = t // block_t
    tmap = (lambda bi, ti: (bi, nt - 1 - ti, 0)) if reverse else (lambda bi, ti: (bi, ti, 0))
    tok = pl.BlockSpec((1, block_t, d), tmap)
    st = pl.BlockSpec((1, d // LANES, LANES, LANES), lambda bi, ti: (bi, 0, 0, 0))
    n_chunks, n_pairs = block_t // SCAN_CHUNK, d // LANES
    return pl.pallas_call(
        functools.partial(_scan_kernel, reverse=reverse, n_chunks=n_chunks, add_to=add_to is not None),
        grid=(b, nt),
        in_specs=[tok] * 6 + [st] + ([tok] if add_to is not None else []),
        out_specs=[tok, st],
        out_shape=[jax.ShapeDtypeStruct((b, t, d), _F32),
                   jax.ShapeDtypeStruct((b, d // LANES, LANES, LANES), _F32)],
        scratch_shapes=[pltpu.VMEM((n_chunks, n_pairs // 2, SCAN_CHUNK + LANES, 2 * LANES), _BF16),
                        pltpu.VMEM((n_chunks, n_pairs // 2, SCAN_CHUNK, 2 * LANES), _F32),
                        pltpu.VMEM((n_chunks, n_pairs // 2, LANES, 2 * LANES), _F32),
                        pltpu.VMEM((n_chunks, n_pairs // 2, LANES, 2 * LANES), _F32)],
        compiler_params=_cparams(("parallel", "arbitrary")),
        name="rwkv_scan_rev" if reverse else "rwkv_scan_fwd",
    )(r, k, v, logd, kk, beta, s0, *([add_to] if add_to is not None else []))


def _layer_norm(x, eps=LN_EPS):
    mu = jnp.mean(x, axis=-1, keepdims=True)
    xc = x - mu
    var = jnp.mean(xc * xc, axis=-1, keepdims=True)
    return xc * lax.rsqrt(var + eps)


def _sigmoid(x):
    return 0.5 * (1.0 + jnp.tanh(0.5 * x))


def _head_sum(x):
    row = lax.broadcasted_iota(jnp.int32, (LANES, LANES), 0) // HEAD
    col = lax.broadcasted_iota(jnp.int32, (LANES, LANES), 1) // HEAD
    ones = (row == col).astype(_BF16)
    hi = x.astype(_BF16)
    lo = (x - hi.astype(_F32)).astype(_BF16)
    slabs = []
    for s in range(0, x.shape[-1], LANES):
        slabs.append(jnp.dot(hi[:, s:s + LANES], ones, preferred_element_type=_F32)
                     + jnp.dot(lo[:, s:s + LANES], ones, preferred_element_type=_F32))
    return jnp.concatenate(slabs, axis=-1)


def _mod_map(n_mod):
    return (lambda bi, ti: (bi, 0, 0)) if n_mod > 1 else (lambda bi, ti: (0, 0, 0))


def _vec_spec(d):
    return pl.BlockSpec((1, d), lambda bi, ti: (0, 0))


def _full_spec(shape):
    return pl.BlockSpec(shape, lambda bi, ti: (0,) * len(shape))


def _tok_spec(tm, d):
    return pl.BlockSpec((1, tm, d), lambda bi, ti: (bi, ti, 0))


def _ada_kernel(c_ref, w_ref, b_ref, o_ref):
    c = c_ref[...]
    o_ref[0] = _bdot(c * _sigmoid(c), w_ref[0]) + b_ref[0]


def _ada_modulation(cc, ada_w, ada_b, *, block_n):
    depth, d, n = ada_w.shape
    rows = cc.shape[0]
    return pl.pallas_call(
        _ada_kernel,
        grid=(depth, n // block_n),
        in_specs=[pl.BlockSpec((rows, d), lambda li, ni: (0, 0)),
                  pl.BlockSpec((1, d, block_n), lambda li, ni: (li, 0, ni)),
                  pl.BlockSpec((1, 1, block_n), lambda li, ni: (li, 0, ni))],
        out_specs=pl.BlockSpec((1, rows, block_n), lambda li, ni: (li, 0, ni)),
        out_shape=jax.ShapeDtypeStruct((depth, rows, n), _F32),
        compiler_params=_cparams(("parallel", "parallel")),
        name="ada_modulation",
    )(cc, ada_w, ada_b.reshape(depth, 1, n))


def _ln_mod_matmul_kernel(x_ref, sh_ref, sc_ref, w_ref, o_ref):
    h = _layer_norm(x_ref[0]) * (1.0 + sc_ref[0]) + sh_ref[0]
    o_ref[0] = _bdot(h, w_ref[...]).astype(o_ref.dtype)


def _ln_mod_matmul(x, shift, scale, w, *, block_t, out_dtype=_F32):
    b, t, d = x.shape
    n = w.shape[1]
    mod = pl.BlockSpec((1, 1, d), _mod_map(shift.shape[0]))
    return pl.pallas_call(
        _ln_mod_matmul_kernel,
        grid=(b, t // block_t),
        in_specs=[_tok_spec(block_t, d), mod, mod, _full_spec((d, n))],
        out_specs=_tok_spec(block_t, n),
        out_shape=jax.ShapeDtypeStruct((b, t, n), out_dtype),
        compiler_params=_cparams(("parallel", "parallel")),
        name="ln_mod_matmul",
    )(x, shift, scale, w)


def _prep_kernel(z_ref, zp_ref, zn_ref, conv_ref, w0_ref, w2f_ref, w2b_ref, a0_ref, a2f_ref, a2b_ref,
                 g2_ref, kk_ref, ka_ref,
                 r_o, v_o, kkn_o, kf_o, kb_o, ldf_o, ldb_o, bf_o, bb_o, g_o):
    z = z_ref[0]
    tm = z.shape[0]
    d = r_o.shape[-1]
    row = lax.broadcasted_iota(jnp.int32, z.shape, 0)
    t_idx = pl.program_id(1)
    before = jnp.where(t_idx == 0, 0.0, zp_ref[0, SUBLANES - 1:SUBLANES, :])
    after = jnp.where(t_idx == pl.num_programs(1) - 1, 0.0, zn_ref[0, 0:1, :])
    up = jnp.where(row == 0, before, pltpu.roll(z, 1, 0))
    dn = jnp.where(row == tm - 1, after, pltpu.roll(z, tm - 1, 0))
    conv = conv_ref[...]
    zs = up * conv[0:1] + z * conv[1:2] + dn * conv[2:3]
    r = zs[:, 0:d]
    k = zs[:, d:2 * d]
    v = zs[:, 2 * d:3 * d]
    tw = jnp.tanh(zs[:, 3 * d:3 * d + LANES])
    la = zs[:, 3 * d + LANES:3 * d + 2 * LANES]
    dg = zs[:, 3 * d + 2 * LANES:3 * d + 3 * LANES]

    def log_decay(w0, w2_ref):
        return -math.exp(-0.5) * _sigmoid(w0 + _bdot(tw, w2_ref[...]))

    w0 = w0_ref[...]
    a0 = a0_ref[...]
    a_f = _sigmoid(a0[0:1] + _bdot(la, a2f_ref[...]))
    a_b = _sigmoid(a0[1:2] + _bdot(la, a2b_ref[...]))
    kkr = k * kk_ref[...]
    kkn = kkr * lax.rsqrt(jnp.maximum(_head_sum(kkr * kkr), 1e-12))
    ka = ka_ref[...]
    r_o[0] = r.astype(_BF16)
    v_o[0] = v.astype(_BF16)
    kkn_o[0] = kkn.astype(_BF16)
    kf_o[0] = (k * (1.0 + (a_f - 1.0) * ka)).astype(_BF16)
    kb_o[0] = (k * (1.0 + (a_b - 1.0) * ka)).astype(_BF16)
    ldf_o[0] = log_decay(w0[0:1], w2f_ref)
    ldb_o[0] = log_decay(w0[1:2], w2b_ref)
    bf_o[0] = (kkn * a_f).astype(_BF16)
    bb_o[0] = (kkn * a_b).astype(_BF16)
    g_o[0] = _bdot(_sigmoid(dg), g2_ref[...]).astype(_BF16)


def _rwkv_prep(z, conv, w0, w2f, w2b, a0, a2f, a2b, g2, k_k, k_a, *, d, block_t):
    b, t, nr = z.shape
    nb = t // block_t
    tiles = block_t // SUBLANES
    prev_tile = pl.BlockSpec((1, SUBLANES, nr), lambda bi, ti: (bi, jnp.maximum(ti * tiles - 1, 0), 0))
    next_tile = pl.BlockSpec((1, SUBLANES, nr), lambda bi, ti: (bi, jnp.minimum((ti + 1) * tiles, nb * tiles - 1), 0))
    f32, b16 = jax.ShapeDtypeStruct((b, t, d), _F32), jax.ShapeDtypeStruct((b, t, d), _BF16)
    return pl.pallas_call(
        _prep_kernel,
        grid=(b, nb),
        in_specs=[_tok_spec(block_t, nr), prev_tile, next_tile, _full_spec(conv.shape), _full_spec(w0.shape),
                  _full_spec(w2f.shape), _full_spec(w2b.shape), _full_spec(a0.shape), _full_spec(a2f.shape),
                  _full_spec(a2b.shape), _full_spec(g2.shape), _vec_spec(d), _vec_spec(d)],
        out_specs=[_tok_spec(block_t, d)] * 10,
        out_shape=[b16, b16, b16, b16, b16, f32, f32, b16, b16, b16],
        compiler_params=_cparams(("parallel", "parallel")),
        name="rwkv_prep",
    )(z, z, z, conv, w0, w2f, w2b, a0, a2f, a2b, g2, k_k, k_a)


def _sgu_kernel(z_ref, lg_ref, lb_ref, w_ref, bias_ref, o_ref, *, chunk):
    z = z_ref[0].astype(_F32)
    ge = 0.5 * z * (1.0 + lax.erf(z * (2.0 ** -0.5)))
    d = o_ref.shape[-1]
    u = ge[:, :d]
    v = _layer_norm(ge[:, d:]) * lg_ref[...] + lb_ref[...]
    bias = bias_ref[...]
    for c in range(z.shape[0] // chunk):
        rows = slice(c * chunk, (c + 1) * chunk)
        for g in range(w_ref.shape[0]):
            cols = slice(g * (d // w_ref.shape[0]), (g + 1) * (d // w_ref.shape[0]))
            mixed = _bdot(w_ref[g], v[rows, cols]) + bias[:, cols]
            o_ref[0, rows, cols] = (u[rows, cols] * mixed).astype(o_ref.dtype)


def _spatial_gating(z_sgu, ln_g, ln_b, w, bias_full, *, block_t):
    b, t, _ = z_sgu.shape
    d = ln_g.shape[-1]
    d2 = 2 * d
    chunk = w.shape[-1]
    return pl.pallas_call(
        functools.partial(_sgu_kernel, chunk=chunk),
        grid=(b, t // block_t),
        in_specs=[_tok_spec(block_t, d2), _vec_spec(d), _vec_spec(d), _full_spec(w.shape),
                  _full_spec(bias_full.shape)],
        out_specs=_tok_spec(block_t, d),
        out_shape=jax.ShapeDtypeStruct((b, t, d), _BF16),
        compiler_params=_cparams(("parallel", "parallel")),
        name="spatial_gating",
    )(z_sgu, ln_g, ln_b, w, bias_full)


def _merge_kernel(y_ref, r_ref, kf_ref, v_ref, g_ref, ys_ref, zg_ref, x_ref, gt_ref,
                  lnxg_ref, lnxb_ref, rk_ref, wa_ref, wb_ref, wo_ref, l1g_ref, l1b_ref, o_ref, *, alpha):
    d = o_ref.shape[-1]
    y = y_ref[0]
    yc = y - _head_sum(y) * (1.0 / HEAD)
    var = _head_sum(yc * yc) * (1.0 / HEAD)
    yn = yc * lax.rsqrt(var + GN_EPS) * lnxg_ref[...] + lnxb_ref[...]
    f32 = lambda ref: ref[0].astype(_F32)
    bonus = _head_sum(f32(r_ref) * f32(kf_ref) * rk_ref[...]) * f32(v_ref)
    y_a = (yn + bonus) * f32(g_ref)
    zg = zg_ref[0].astype(_F32)
    merged = _sigmoid(zg[:, :d]) * _bdot(y_a, wa_ref[...]) + _sigmoid(zg[:, d:]) * _bdot(ys_ref[0], wb_ref[...])
    out = _bdot(merged, wo_ref[...])
    o_ref[0] = _layer_norm(alpha * x_ref[0] + gt_ref[0] * out) * l1g_ref[...] + l1b_ref[...]


def _merge(y, r, kf, v, g, ys, z_sg, x, gate, lnx_g, lnx_b, r_k, wa, wb, wo, l1g, l1b, *, alpha, block_t):
    b, t, d = x.shape
    tok = _tok_spec(block_t, d)
    vec = _vec_spec(d)
    mat = _full_spec((d, d))
    gate_half = pl.BlockSpec((1, block_t, 2 * d), lambda bi, ti: (bi, ti, 1))
    return pl.pallas_call(
        functools.partial(_merge_kernel, alpha=alpha),
        grid=(b, t // block_t),
        in_specs=[tok] * 6 + [gate_half, tok, pl.BlockSpec((1, 1, d), _mod_map(gate.shape[0])),
                  vec, vec, vec, mat, mat, mat, vec, vec],
        out_specs=tok,
        out_shape=jax.ShapeDtypeStruct((b, t, d), _F32),
        compiler_params=_cparams(("parallel", "parallel")),
        name="branch_merge",
    )(y, r, kf, v, g, ys, z_sg, x, gate, lnx_g, lnx_b, r_k, wa, wb, wo, l1g, l1b)


def _split_bf16(x):
    hi = x.astype(_BF16)
    return hi, (x - hi.astype(_F32)).astype(_BF16)


def _router_kernel(x_ref, sh_ref, sc_ref, rw_ref, h_ref, aff_ref, *, n_exp):
    h = _layer_norm(x_ref[0]) * (1.0 + sc_ref[0]) + sh_ref[0]
    h_ref[0] = h.astype(_BF16)
    h_hi, h_lo = _split_bf16(h)
    w_hi, w_lo = _split_bf16(rw_ref[...])
    dot = lambda a, b: jnp.dot(a, b, preferred_element_type=_F32)
    logits = dot(h_hi, w_hi) + (dot(h_hi, w_lo) + dot(h_lo, w_hi))
    lane = lax.broadcasted_iota(jnp.int32, logits.shape, 1)
    logits = jnp.where(lane < n_exp, logits, -jnp.inf)
    e = jnp.exp(logits - jnp.max(logits, axis=-1, keepdims=True))
    aff_ref[0] = e / jnp.sum(e, axis=-1, keepdims=True)


def _router(x, shift, scale, rw_pad, *, n_exp, block_t):
    b, t, d = x.shape
    mod = pl.BlockSpec((1, 1, d), _mod_map(shift.shape[0]))
    return pl.pallas_call(
        functools.partial(_router_kernel, n_exp=n_exp),
        grid=(b, t // block_t),
        in_specs=[_tok_spec(block_t, d), mod, mod, _full_spec(rw_pad.shape)],
        out_specs=[_tok_spec(block_t, d), _tok_spec(block_t, LANES)],
        out_shape=[jax.ShapeDtypeStruct((b, t, d), _BF16), jax.ShapeDtypeStruct((b, t, LANES), _F32)],
        compiler_params=_cparams(("parallel", "parallel")),
        name="moe_router",
    )(x, shift, scale, rw_pad)


def _count_before(mask):
    t = mask.shape[1]
    tile = min(t, PREFIX_TILE)
    earlier = (lax.broadcasted_iota(jnp.int32, (tile, tile), 0)
               < lax.broadcasted_iota(jnp.int32, (tile, tile), 1)).astype(_BF16)
    pieces, total = [], jnp.zeros((mask.shape[0], 1), _F32)
    for t0 in range(0, t, tile):
        m = mask[:, t0:t0 + tile]
        pieces.append(jnp.dot(m.astype(_BF16), earlier, preferred_element_type=_F32) + total)
        total = total + jnp.sum(m, axis=1, keepdims=True)
    return jnp.concatenate(pieces, axis=1)


def _select_kernel(aff_ref, slot_ref, *, cap):
    a = aff_ref[0]
    key = lax.bitcast_convert_type(a, jnp.int32)
    thr = jnp.zeros((a.shape[0], 1), jnp.int32)
    for bit in range(30, -1, -1):
        cand = thr | (1 << bit)
        count = jnp.sum((key >= cand).astype(_F32), axis=1, keepdims=True)
        thr = jnp.where(count >= cap, cand, thr)
    above = key > thr
    tied = key == thr
    need = cap - jnp.sum(above.astype(_F32), axis=1, keepdims=True)
    chosen = jnp.logical_or(above, jnp.logical_and(tied, _count_before(tied.astype(_F32)) < need))
    slot_ref[0] = jnp.where(chosen, _count_before(chosen.astype(_F32)), -1.0)


def _expert_choice_slots(aff_t, *, cap):
    b, n_exp, t = aff_t.shape
    return pl.pallas_call(
        functools.partial(_select_kernel, cap=cap),
        grid=(b,),
        in_specs=[pl.BlockSpec((1, n_exp, t), lambda bi: (bi, 0, 0))],
        out_specs=pl.BlockSpec((1, n_exp, t), lambda bi: (bi, 0, 0)),
        out_shape=jax.ShapeDtypeStruct((b, n_exp, t), _F32),
        compiler_params=_cparams(("parallel",)),
        name="expert_choice_slots",
    )(aff_t)


def _moe_kernel(*refs, n_sets):
    ins, rest = refs[:3 * n_sets], refs[3 * n_sets:]
    w1_ref, w3_ref, w2_ref = rest[:3]
    outs = rest[3:3 + n_sets]
    xe_ref, ye_ref = rest[3 + n_sets:5 + n_sets]
    p_refs = rest[5 + n_sets:]
    e_idx = pl.program_id(1)
    f_idx = pl.program_id(2)
    caps = [p.shape[0] for p in p_refs]
    offs = [sum(caps[:i]) for i in range(n_sets)]

    @pl.when(jnp.logical_and(e_idx == 0, f_idx == 0))
    def _():
        for o_ref in outs:
            o_ref[...] = jnp.zeros_like(o_ref)

    @pl.when(f_idx == 0)
    def _():
        for i in range(n_sets):
            h_ref, slot_ref = ins[3 * i:3 * i + 2]
            cap, t = p_refs[i].shape
            slot = lax.broadcasted_iota(jnp.int32, (cap, t), 0).astype(_F32)
            p = (slot_ref[0, 0] == slot).astype(_BF16)
            p_refs[i][...] = p
            xe_ref[offs[i]:offs[i] + cap, :] = jnp.dot(p, h_ref[0], preferred_element_type=_F32).astype(_BF16)
        ye_ref[...] = jnp.zeros_like(ye_ref)

    xe = xe_ref[...]
    h1 = jnp.dot(xe, w1_ref[0, 0], preferred_element_type=_F32)
    h3 = jnp.dot(xe, w3_ref[0, 0], preferred_element_type=_F32)
    ye_ref[...] += _bdot(h1 * _sigmoid(h1) * h3, w2_ref[0, 0])

    @pl.when(f_idx == pl.num_programs(2) - 1)
    def _():
        for i in range(n_sets):
            aff_ref, o_ref = ins[3 * i + 2], outs[i]
            cap, t = p_refs[i].shape
            blk = min(t, SCATTER_ROWS)
            ye = ye_ref[offs[i]:offs[i] + cap, :].astype(_BF16)
            this_expert = lax.broadcasted_iota(jnp.int32, (blk, LANES), 1) == e_idx
            for t0 in range(0, t, blk):
                back = lax.dot_general(p_refs[i][:, t0:t0 + blk], ye, (((0,), (0,)), ((), ())),
                                       preferred_element_type=_F32)
                gate = jnp.sum(jnp.where(this_expert, aff_ref[0, t0:t0 + blk, :], 0.0), axis=1, keepdims=True)
                o_ref[0, t0:t0 + blk, :] += gate * back


def _expert_choice_ffn(sets, w1, w3, w2, *, layer, block_f):
    _, n_exp, d, d_ff = w1.shape
    b = sets[0][0].shape[0]
    caps = [CAP_FACTOR * h.shape[1] // n_exp for h, _, _ in sets]
    in_specs, out_specs, out_shape, operands = [], [], [], []
    for h, slot, aff in sets:
        t = h.shape[1]
        in_specs += [pl.BlockSpec((1, t, d), lambda bi, ei, fi: (bi, 0, 0)),
                     pl.BlockSpec((1, 1, 1, t), lambda bi, ei, fi: (bi, ei, 0, 0)),
                     pl.BlockSpec((1, t, LANES), lambda bi, ei, fi: (bi, 0, 0))]
        out_specs.append(pl.BlockSpec((1, t, d), lambda bi, ei, fi: (bi, 0, 0)))
        out_shape.append(jax.ShapeDtypeStruct((b, t, d), _F32))
        operands += [h, slot, aff]
    in_specs += [pl.BlockSpec((1, 1, d, block_f), lambda bi, ei, fi: (layer, ei, 0, fi)),
                 pl.BlockSpec((1, 1, d, block_f), lambda bi, ei, fi: (layer, ei, 0, fi)),
                 pl.BlockSpec((1, 1, block_f, d), lambda bi, ei, fi: (layer, ei, fi, 0))]
    return pl.pallas_call(
        functools.partial(_moe_kernel, n_sets=len(sets)),
        grid=(b, n_exp, d_ff // block_f),
        in_specs=in_specs,
        out_specs=out_specs,
        out_shape=out_shape,
        scratch_shapes=[pltpu.VMEM((sum(caps), d), _BF16), pltpu.VMEM((sum(caps), d), _F32)]
        + [pltpu.VMEM((cap, h.shape[1]), _BF16) for cap, (h, _, _) in zip(caps, sets)],
        compiler_params=_cparams(("parallel", "arbitrary", "arbitrary")),
        name="expert_choice_ffn",
    )(*operands, w1, w3, w2)


def _post_norm_kernel(x_ref, y_ref, gt_ref, g_ref, b_ref, o_ref, *, alpha):
    o_ref[0] = _layer_norm(alpha * x_ref[0] + gt_ref[0] * y_ref[0]) * g_ref[...] + b_ref[...]


def _post_norm(x, y, gate, g, bias, *, alpha, block_t):
    b, t, d = x.shape
    tok = _tok_spec(block_t, d)
    return pl.pallas_call(
        functools.partial(_post_norm_kernel, alpha=alpha),
        grid=(b, t // block_t),
        in_specs=[tok, tok, pl.BlockSpec((1, 1, d), _mod_map(gate.shape[0])), _vec_spec(d), _vec_spec(d)],
        out_specs=tok,
        out_shape=jax.ShapeDtypeStruct((b, t, d), _F32),
        compiler_params=_cparams(("parallel", "parallel")),
        name="post_norm",
    )(x, y, gate, g, bias)


TOK_BLOCK = 512
TOK_BLOCK_WIDE = 256
FF_BLOCK = 1024
SCATTER_ROWS = 512
PREFIX_TILE = 256


def _block_t(t, want):
    return want if t % want == 0 else t


def _pad_rows(w, top):
    out = jnp.zeros((LANES, w.shape[1]), w.dtype)
    return lax.dynamic_update_slice(out, w, (top, 0))


def kernel(x, c, ctx, c_ctx, ada_w, ada_b, w_in, shift_conv, w0, w2, a0, a2, g2, k_k, k_a, r_k, lnx_g, lnx_b,
           sgu_ln_g, sgu_ln_b, sgu_w, sgu_b, w_branch_a, w_branch_b, w_out, ln1_g, ln1_b, router_w, exp_w1,
           exp_w3, exp_w2, ln2_g, ln2_b):
    bsz, seq, d = x.shape
    depth = w_in.shape[0]
    n_rwkv = shift_conv.shape[-1]
    r_w, r_a, r_g = w2.shape[2], a2.shape[2], g2.shape[1]
    n_exp = router_w.shape[-1]
    assert 2 * r_w == LANES and 2 * r_a == LANES and r_g == LANES and n_rwkv == 3 * d + 3 * LANES
    assert r_k.shape[2] == HEAD and d % LANES == 0
    alpha = (2 * depth) ** 0.25

    n_rows = -(-(bsz + 1) // 8) * 8
    cc = jnp.zeros((n_rows, d), _F32).at[:bsz].set(c).at[bsz].set(c_ctx)
    mods = _ada_modulation(cc, ada_w, ada_b, block_n=d)

    s_zero = jnp.zeros((bsz, d // LANES, LANES, LANES), _F32)
    vec = lambda p: p.reshape(1, d)

    ew1, ew3, ew2 = exp_w1.astype(_BF16), exp_w3.astype(_BF16), exp_w2.astype(_BF16)

    for l in range(depth):
        mod_x = mods[l, :bsz].reshape(bsz, 1, 6, d)
        mod_c = mods[l, bsz].reshape(1, 1, 6, d)
        lat_mod = [mod_x[:, :, i] for i in range(6)]
        ctx_mod = [mod_c[:, :, i] for i in range(6)]
        w_rwkv = w_in[l, :, :n_rwkv].astype(_BF16)
        w_sgu_gate = w_in[l, :, n_rwkv:].astype(_BF16)
        prep_w = dict(conv=shift_conv[l], w0=w0[l],
                      w2f=_pad_rows(w2[l, 0], 0).astype(_BF16), w2b=_pad_rows(w2[l, 1], r_w).astype(_BF16),
                      a0=a0[l], a2f=_pad_rows(a2[l, 0], 0).astype(_BF16),
                      a2b=_pad_rows(a2[l, 1], r_a).astype(_BF16), g2=g2[l].astype(_BF16),
                      k_k=vec(k_k[l]), k_a=vec(k_a[l]))
        chunk = sgu_w.shape[-1]
        sgu_bias = jnp.repeat(sgu_b[l].T, d // sgu_w.shape[1], axis=1)
        mix_w = (vec(lnx_g[l]), vec(lnx_b[l]), r_k[l].reshape(1, d), w_branch_a[l].astype(_BF16),
                 w_branch_b[l].astype(_BF16), w_out[l].astype(_BF16), vec(ln1_g[l]), vec(ln1_b[l]))
        rw_pad = jnp.zeros((d, LANES), _F32).at[:, :n_exp].set(router_w[l])

        def mixer(tok, mod, s0_f, s0_b, full):
            t = tok.shape[1]
            bt = _block_t(t, TOK_BLOCK)
            z_r = _ln_mod_matmul(tok, mod[0], mod[1], w_rwkv, block_t=bt)
            r, v, kk, kf, kb, ldf, ldb, bef, beb, g = _rwkv_prep(z_r, d=d, block_t=_block_t(t, TOK_BLOCK_WIDE), **prep_w)
            y_f, s_f = _rwkv_scan(r, kf, v, ldf, kk, bef, s0_f, reverse=False, block_t=_block_t(t, TOK_BLOCK))
            y, s_b = _rwkv_scan(r, kb, v, ldb, kk, beb, s0_b, reverse=True, block_t=_block_t(t, TOK_BLOCK),
                                add_to=y_f)
            if not full:
                return None, s_f, s_b
            z_sg = _ln_mod_matmul(tok, mod[0], mod[1], w_sgu_gate, block_t=bt,
                                  out_dtype=_BF16)
            y_s = _spatial_gating(z_sg, vec(sgu_ln_g[l]), vec(sgu_ln_b[l]), sgu_w[l].astype(_BF16), sgu_bias,
                                  block_t=_block_t(t, 2 * chunk))
            out = _merge(y, r, kf, v, g, y_s, z_sg, tok, mod[2], *mix_w, alpha=alpha,
                         block_t=_block_t(t, TOK_BLOCK_WIDE))
            return out, s_f, s_b

        def moe(streams):
            sets = []
            for tok, mod in streams:
                h, aff = _router(tok, mod[3], mod[4], rw_pad, n_exp=n_exp, block_t=_block_t(tok.shape[1], TOK_BLOCK))
                aff_t = jnp.swapaxes(aff[:, :, :n_exp], 1, 2)
                slots = _expert_choice_slots(aff_t, cap=CAP_FACTOR * tok.shape[1] // n_exp)
                sets.append((h, slots[:, :, None, :], aff))
            ys = _expert_choice_ffn(sets, ew1, ew3, ew2, layer=l, block_f=_block_t(ew1.shape[-1], FF_BLOCK))
            return [_post_norm(tok, y, mod[5], vec(ln2_g[l]), vec(ln2_b[l]), alpha=alpha,
                               block_t=_block_t(tok.shape[1], TOK_BLOCK)) for (tok, mod), y in zip(streams, ys)]

        last = l == depth - 1
        ctx_mixed, s_f, s_b = mixer(ctx, ctx_mod, s_zero, s_zero, full=not last)
        x, _, _ = mixer(x, lat_mod, s_f, s_b, full=True)
        if last:
            x, = moe([(x, lat_mod)])
        else:
            x, ctx = moe([(x, lat_mod), (ctx_mixed, ctx_mod)])
    return x
```
